```python
import math
import jax, jax.numpy as jnp
from jax import lax
import numpy as np

D_MODEL = 2048
BATCH = 8
SEQ = 2048
DEPTH = 1

CHUNK = 64
D_MIX = D_MODEL
D_A = D_MIX // 2
D_B = D_MIX - D_A
N_HEADS_A = 8
HEAD_DIM_A = D_A // N_HEADS_A
GMLP_BLOCK = 128
POOL_WINDOWS = (2, 4, 8, 16)
N_POOL_GROUPS = len(POOL_WINDOWS)
POOL_GROUP_DIM = D_B // N_POOL_GROUPS
N_EXPERTS = 256
TOP_K = 8
N_EXPERT_GROUPS = 8
TOPK_GROUPS = 4
ROUTED_SCALE = 2.5
D_EXPERT = 512
D_SHARED = 512
EXPERT_BLOCK = 128
LN_EPS = 1e-5
DEEPNORM_ALPHA = (2.0 * DEPTH) ** 0.25
DEEPNORM_BETA = (8.0 * DEPTH) ** -0.25

kernel_name = "hybrid_gmlp_pool_moe_deepnorm_adaln"


def _layer_norm(x, g, b):
    xf = x.astype(jnp.float32)
    mu = jnp.mean(xf, axis=-1, keepdims=True)
    var = jnp.mean(jnp.square(xf - mu), axis=-1, keepdims=True)
    y = (xf - mu) * lax.rsqrt(var + LN_EPS)
    return (y * g.astype(jnp.float32) + b.astype(jnp.float32)).astype(x.dtype)


def _gmlp_mixer(u, v, ln_g, ln_b, w_spatial, b_spatial):
    B, S, _ = v.shape
    v = _layer_norm(v, ln_g, ln_b)
    pos = jnp.arange(GMLP_BLOCK)
    mask = (pos[None, :] // CHUNK) <= (pos[:, None] // CHUNK)
    ws = jnp.where(mask[None], w_spatial, jnp.zeros((), w_spatial.dtype))
    vb = v.reshape(B, S // GMLP_BLOCK, GMLP_BLOCK, N_HEADS_A, HEAD_DIM_A)
    mixed = jnp.einsum('hij,bnjhd->bnihd', ws, vb) + b_spatial.T[None, None, :, :, None]
    return u * mixed.reshape(B, S, D_A)


def _pool_mixer(z, w_pool, b_pool, pool_scale):
    B, S, _ = z.shape
    zf = z.astype(jnp.float32)
    cs = jnp.concatenate([jnp.zeros((B, 1, D_B), jnp.float32), jnp.cumsum(zf, axis=1)], axis=1)
    t = jnp.arange(S)
    pooled = []
    for gi, w in enumerate(POOL_WINDOWS):
        csg = cs[..., gi * POOL_GROUP_DIM:(gi + 1) * POOL_GROUP_DIM]
        lo = jnp.maximum(t + 1 - w, 0)
        cnt = (t + 1 - lo).astype(jnp.float32)
        pooled.append((csg[:, 1:, :] - csg[:, lo, :]) / cnt[None, :, None])
    pooled = (jnp.concatenate(pooled, axis=-1) - zf).astype(z.dtype)
    pg = pooled.reshape(B, S, N_POOL_GROUPS, POOL_GROUP_DIM)
    y = jnp.einsum('bsgc,gcd->bsgd', pg, w_pool) + b_pool[None, None]
    return y.reshape(B, S, D_B) * pool_scale


def _moe(h, w_router, router_bias, w1, w3, w2, ws1, ws3, ws2):
    B, S, D = h.shape
    T = B * S
    hf = h.reshape(T, D)
    scores = jax.nn.sigmoid(jnp.dot(hf.astype(jnp.float32), w_router.astype(jnp.float32)))
    sel = scores + router_bias.astype(jnp.float32)
    grp = sel.reshape(T, N_EXPERT_GROUPS, N_EXPERTS // N_EXPERT_GROUPS)
    grp_score = jnp.sum(lax.top_k(grp, 2)[0], axis=-1)
    _, grp_idx = lax.top_k(grp_score, TOPK_GROUPS)
    grp_mask = jnp.sum(jax.nn.one_hot(grp_idx, N_EXPERT_GROUPS, dtype=jnp.float32), axis=1) > 0
    exp_mask = jnp.repeat(grp_mask, N_EXPERTS // N_EXPERT_GROUPS, axis=1)
    _, top_idx = lax.top_k(jnp.where(exp_mask, sel, -jnp.inf), TOP_K)
    top_s = jnp.take_along_axis(scores, top_idx, axis=1)
    gates = top_s / jnp.sum(top_s, axis=-1, keepdims=True) * ROUTED_SCALE

    A = T * TOP_K
    M = EXPERT_BLOCK
    NB = (A + M - 1) // M + N_EXPERTS
    P = NB * M
    e_flat = top_idx.reshape(A).astype(jnp.int32)
    tok_flat = jnp.arange(A, dtype=jnp.int32) // TOP_K
    g_flat = gates.reshape(A)
    order = jnp.argsort(e_flat)
    e_sorted = e_flat[order]
    counts = jnp.bincount(e_flat, length=N_EXPERTS).astype(jnp.int32)
    starts = jnp.cumsum(counts) - counts
    pcounts = (counts + M - 1) // M * M
    pends = jnp.cumsum(pcounts)
    pstarts = pends - pcounts
    dest = pstarts[e_sorted] + (jnp.arange(A, dtype=jnp.int32) - starts[e_sorted])
    buf_tok = jnp.zeros((P,), jnp.int32).at[dest].set(tok_flat[order])
    buf_gate = jnp.zeros((P,), jnp.float32).at[dest].set(g_flat[order])
    block_e = jnp.searchsorted(pends, jnp.arange(NB, dtype=jnp.int32) * M, side='right')
    block_e = jnp.minimum(block_e, N_EXPERTS - 1).astype(jnp.int32)

    def expert_block(args):
        tok, gate, e = args
        xb = hf[tok]
        y = (jax.nn.silu(xb @ w1[e]) * (xb @ w3[e])) @ w2[e]
        return y * gate[:, None].astype(y.dtype)

    ys = lax.map(expert_block, (buf_tok.reshape(NB, M), buf_gate.reshape(NB, M), block_e))
    routed = jnp.zeros((T, D), jnp.float32).at[buf_tok].add(ys.reshape(P, D).astype(jnp.float32))
    shared = (jax.nn.silu(hf @ ws1) * (hf @ ws3)) @ ws2
    return (shared + routed.astype(h.dtype)).reshape(B, S, D)


def setup_inputs(seed: int = 0) -> dict:
    key = jax.random.key(seed)
    ks = jax.random.split(key, 32)
    f32 = jnp.float32

    def nrm(k, shape, scale):
        return jax.random.normal(k, shape, f32) * scale

    L, D, E = DEPTH, D_MODEL, N_EXPERTS
    return {
        "x": nrm(ks[0], (BATCH, SEQ, D), 1.0),
        "c": nrm(ks[1], (BATCH, D), 1.0),
        "w_ada": nrm(ks[2], (L, D, 6 * D), D ** -0.5),
        "b_ada": nrm(ks[3], (L, 6 * D), 0.01),
        "w_in": nrm(ks[4], (L, D, 2 * D_A + D_B), D ** -0.5),
        "b_in": nrm(ks[5], (L, 2 * D_A + D_B), 0.01),
        "ln_v_g": 1.0 + nrm(ks[6], (L, D_A), 0.01),
        "ln_v_b": nrm(ks[7], (L, D_A), 0.01),
        "w_spatial": nrm(ks[8], (L, N_HEADS_A, GMLP_BLOCK, GMLP_BLOCK), GMLP_BLOCK ** -0.5),
        "b_spatial": 1.0 + nrm(ks[9], (L, N_HEADS_A, GMLP_BLOCK), 0.01),
        "w_pool": nrm(ks[10], (L, N_POOL_GROUPS, POOL_GROUP_DIM, POOL_GROUP_DIM), POOL_GROUP_DIM ** -0.5),
        "b_pool": nrm(ks[11], (L, N_POOL_GROUPS, POOL_GROUP_DIM), 0.01),
        "pool_scale": 1.0 + nrm(ks[12], (L, D_B), 0.01),
        "w_out": nrm(ks[13], (L, D_MIX, D), D_MIX ** -0.5 * DEEPNORM_BETA),
        "ln1_g": 1.0 + nrm(ks[14], (L, D), 0.01),
        "ln1_b": nrm(ks[15], (L, D), 0.01),
        "w_router": nrm(ks[16], (L, D, E), D ** -0.5),
        "router_bias": nrm(ks[17], (L, E), 0.01),
        "w1": nrm(ks[18], (L, E, D, D_EXPERT), D ** -0.5),
        "w3": nrm(ks[19], (L, E, D, D_EXPERT), D ** -0.5),
        "w2": nrm(ks[20], (L, E, D_EXPERT, D), D_EXPERT ** -0.5 * DEEPNORM_BETA),
        "ws1": nrm(ks[21], (L, D, D_SHARED), D ** -0.5),
        "ws3": nrm(ks[22], (L, D, D_SHARED), D ** -0.5),
        "ws2": nrm(ks[23], (L, D_SHARED, D), D_SHARED ** -0.5 * DEEPNORM_BETA),
        "ln2_g": 1.0 + nrm(ks[24], (L, D), 0.01),
        "ln2_b": nrm(ks[25], (L, D), 0.01),
    }


def reference(x, c, w_ada, b_ada, w_in, b_in, ln_v_g, ln_v_b, w_spatial, b_spatial,
              w_pool, b_pool, pool_scale, w_out, ln1_g, ln1_b, w_router, router_bias,
              w1, w3, w2, ws1, ws3, ws2, ln2_g, ln2_b):
    B, S, D = x.shape
    for l in range(DEPTH):
        ada = (jax.nn.silu(c) @ w_ada[l] + b_ada[l]).reshape(B, 6, D)
        shift1, scale1, gate1 = ada[:, 0, None], ada[:, 1, None], ada[:, 2, None]
        shift2, scale2, gate2 = ada[:, 3, None], ada[:, 4, None], ada[:, 5, None]

        h = x * (1 + scale1) + shift1
        proj = h @ w_in[l] + b_in[l]
        u = jax.nn.gelu(proj[..., :D_A], approximate=False)
        v = jax.nn.gelu(proj[..., D_A:2 * D_A], approximate=False)
        a_out = _gmlp_mixer(u, v, ln_v_g[l], ln_v_b[l], w_spatial[l], b_spatial[l])
        b_out = _pool_mixer(proj[..., 2 * D_A:], w_pool[l], b_pool[l], pool_scale[l])
        mix = jnp.concatenate([a_out, b_out], axis=-1) @ w_out[l]
        x = _layer_norm(DEEPNORM_ALPHA * x + gate1 * mix, ln1_g[l], ln1_b[l])

        h = x * (1 + scale2) + shift2
        ffn = _moe(h, w_router[l], router_bias[l], w1[l], w3[l], w2[l], ws1[l], ws3[l], ws2[l])
        x = _layer_norm(DEEPNORM_ALPHA * x + gate2 * ffn, ln2_g[l], ln2_b[l])
    return x
```

```python
import functools

import jax
import jax.numpy as jnp
from jax import lax
from jax.experimental import pallas as pl
from jax.experimental.pallas import tpu as pltpu

F32 = jnp.float32
BF16 = jnp.bfloat16
U32 = jnp.uint32
I32 = jnp.int32

N_HEADS_A = 8
GMLP_BLOCK = 128
CHUNK = 64
POOL_WINDOWS = (2, 4, 8, 16)
POOL_HISTORY = 16
N_EXPERT_GROUPS = 8
TOPK_GROUPS = 4
TOP_K = 8
ROUTED_SCALE = 2.5
LN_EPS = 1e-5
DEPTH = 1
DEEPNORM_ALPHA = (2.0 * DEPTH) ** 0.25

LANES = 128
VMEM_LIMIT_BYTES = 56 * 1024 * 1024
ADA_TN = 1024
MIX_TS = 256
OUT_TM = 256
ROUTE_TR = 256
GMM_TB = 256
FIN_TM = 256


def _dot(a, b):
    return jnp.dot(a, b, preferred_element_type=F32)


def _split_bf16(a):
    hi = a.astype(BF16)
    lo = (a - hi.astype(F32)).astype(BF16)
    return hi, lo


def _gelu(x):
    return 0.5 * x * (1.0 + lax.erf(x * 0.7071067811865476))


def _silu(x):
    return x * jax.nn.sigmoid(x)


def _pack_bf16_pair(a, b):
    au = pltpu.bitcast(a.astype(BF16).astype(F32), U32)
    bu = pltpu.bitcast(b.astype(BF16).astype(F32), U32)
    return (au & jnp.uint32(0xFFFF0000)) | (bu >> 16)


def _unpack_bf16_pair(w):
    a = pltpu.bitcast(w & jnp.uint32(0xFFFF0000), F32)
    b = pltpu.bitcast(w << 16, F32)
    return a, b


def _ada_kernel(c_ref, w_ref, b_ref, o_ref):
    s = _silu(c_ref[...])
    sh, sl = _split_bf16(s)
    wh, wl = _split_bf16(w_ref[...])
    o_ref[...] = _dot(sh, wh) + _dot(sl, wh) + _dot(sh, wl) + b_ref[...]


def _ada_call(c, w_ada, b_ada):
    bsz, d = c.shape
    n = w_ada.shape[1]
    return pl.pallas_call(
        _ada_kernel,
        grid=(n // ADA_TN,),
        in_specs=[
            pl.BlockSpec((bsz, d), lambda j: (0, 0)),
            pl.BlockSpec((d, ADA_TN), lambda j: (0, j)),
            pl.BlockSpec((1, ADA_TN), lambda j: (0, j)),
        ],
        out_specs=pl.BlockSpec((bsz, ADA_TN), lambda j: (0, j)),
        out_shape=jax.ShapeDtypeStruct((bsz, n), F32),
        compiler_params=pltpu.CompilerParams(
            dimension_semantics=("arbitrary",), vmem_limit_bytes=VMEM_LIMIT_BYTES),
        name="ada",
    )(c, w_ada, b_ada.reshape(1, n))


def _mix_kernel(x_ref, ada_ref, win_ref, bin_ref, lng_ref, lnb_ref, wsp_ref, bsp_ref,
                wpool_ref, bpool_ref, pscale_ref, o_ref, carry_ref, *, d_a, d_g):
    j = pl.program_id(1)
    ts = x_ref.shape[1]
    x = x_ref[0]
    shift = ada_ref[0, 0:1, :]
    scale = ada_ref[0, 1:2, :]
    h = (x * (1.0 + scale) + shift).astype(BF16)
    proj = _dot(h, win_ref[...]) + bin_ref[...]

    u = _gelu(proj[:, :d_a])
    v = _gelu(proj[:, d_a:2 * d_a])
    mu = jnp.mean(v, axis=-1, keepdims=True)
    vc = v - mu
    var = jnp.mean(vc * vc, axis=-1, keepdims=True)
    vn = (vc * lax.rsqrt(var + LN_EPS) * lng_ref[...] + lnb_ref[...]).astype(BF16)
    row = lax.broadcasted_iota(I32, (GMLP_BLOCK, GMLP_BLOCK), 0)
    col = lax.broadcasted_iota(I32, (GMLP_BLOCK, GMLP_BLOCK), 1)
    causal = (col // CHUNK) <= (row // CHUNK)
    hd = d_a // N_HEADS_A
    for head in range(N_HEADS_A):
        w = jnp.where(causal, wsp_ref[head], 0.0).astype(BF16)
        bias = bsp_ref[:, head:head + 1]
        cs = slice(head * hd, (head + 1) * hd)
        for n in range(ts // GMLP_BLOCK):
            rs = slice(n * GMLP_BLOCK, (n + 1) * GMLP_BLOCK)
            mixed = _dot(w, vn[rs, cs]) + bias
            o_ref[0, rs, cs] = (u[rs, cs] * mixed).astype(BF16)

    z = proj[:, 2 * d_a:]

    @pl.when(j == 0)
    def _():
        carry_ref[...] = jnp.zeros_like(carry_ref)

    ext = jnp.concatenate([carry_ref[...], z], axis=0)
    carry_ref[...] = z[ts - POOL_HISTORY:, :]
    tpos = j * ts + lax.broadcasted_iota(I32, (ts, 1), 0)
    for g, win in enumerate(POOL_WINDOWS):
        gs = slice(g * d_g, (g + 1) * d_g)
        s = ext[:, gs]
        sh = 1
        while sh < win:
            s = s + pltpu.roll(s, sh, axis=0)
            sh *= 2
        cnt = jnp.minimum(tpos + 1, win).astype(F32)
        pooled = s[POOL_HISTORY:, :] / cnt - z[:, gs]
        y = _dot(pooled.astype(BF16), wpool_ref[g]) + bpool_ref[g:g + 1, :]
        o_ref[0, :, d_a + g * d_g:d_a + (g + 1) * d_g] = (y * pscale_ref[:, gs]).astype(BF16)


def _mix_call(x, ada3, w_in_b, b_in, ln_v_g, ln_v_b, w_spatial, b_spatial_t, w_pool_b, b_pool, pool_scale):
    bsz, seq, d = x.shape
    n_proj = w_in_b.shape[1]
    d_a = ln_v_g.shape[0]
    d_b = pool_scale.shape[0]
    n_g, d_g, _ = w_pool_b.shape
    ts = MIX_TS
    const2 = lambda b, j: (0, 0)
    const3 = lambda b, j: (0, 0, 0)
    return pl.pallas_call(
        functools.partial(_mix_kernel, d_a=d_a, d_g=d_g),
        grid=(bsz, seq // ts),
        in_specs=[
            pl.BlockSpec((1, ts, d), lambda b, j: (b, j, 0)),
            pl.BlockSpec((1, 6, d), lambda b, j: (b, 0, 0)),
            pl.BlockSpec((d, n_proj), const2),
            pl.BlockSpec((1, n_proj), const2),
            pl.BlockSpec((1, d_a), const2),
            pl.BlockSpec((1, d_a), const2),
            pl.BlockSpec((N_HEADS_A, GMLP_BLOCK, GMLP_BLOCK), const3),
            pl.BlockSpec((GMLP_BLOCK, N_HEADS_A), const2),
            pl.BlockSpec((n_g, d_g, d_g), const3),
            pl.BlockSpec((n_g, d_g), const2),
            pl.BlockSpec((1, d_b), const2),
        ],
        out_specs=pl.BlockSpec((1, ts, d_a + d_b), lambda b, j: (b, j, 0)),
        out_shape=jax.ShapeDtypeStruct((bsz, seq, d_a + d_b), BF16),
        scratch_shapes=[pltpu.VMEM((POOL_HISTORY, d_b), F32)],
        compiler_params=pltpu.CompilerParams(
            dimension_semantics=("arbitrary", "arbitrary"), vmem_limit_bytes=VMEM_LIMIT_BYTES),
        name="mix",
    )(x, ada3, w_in_b, b_in.reshape(1, n_proj), ln_v_g.reshape(1, d_a), ln_v_b.reshape(1, d_a),
      w_spatial, b_spatial_t, w_pool_b, b_pool, pool_scale.reshape(1, d_b))


def _layer_norm_rows(y, g, b):
    mu = jnp.mean(y, axis=-1, keepdims=True)
    yc = y - mu
    var = jnp.mean(yc * yc, axis=-1, keepdims=True)
    return yc * lax.rsqrt(var + LN_EPS) * g + b


def _out_kernel(cat_ref, x_ref, ada_ref, wout_ref, g_ref, b_ref, wrh_ref, wrl_ref,
                x1_ref, h2p_ref, lgt_ref):
    half = h2p_ref.shape[1]
    mix = _dot(cat_ref[...], wout_ref[...])
    gate1 = ada_ref[0, 2:3, :]
    x1 = _layer_norm_rows(DEEPNORM_ALPHA * x_ref[...] + gate1 * mix, g_ref[...], b_ref[...])
    x1_ref[...] = x1
    h2 = x1 * (1.0 + ada_ref[0, 4:5, :]) + ada_ref[0, 3:4, :]
    h2p_ref[...] = _pack_bf16_pair(h2[:, :half], h2[:, half:])
    hh, hl = _split_bf16(h2)
    nt = (((1,), (1,)), ((), ()))
    wrh = wrh_ref[...]
    lgt_ref[...] = (lax.dot_general(wrh, hh, nt, preferred_element_type=F32)
                    + lax.dot_general(wrh, hl, nt, preferred_element_type=F32)
                    + lax.dot_general(wrl_ref[...], hh, nt, preferred_element_type=F32))


def _out_call(cat, x2d, ada3, w_out_b, ln1_g, ln1_b, wr_hi_t, wr_lo_t, seq):
    t, d = x2d.shape
    e = wr_hi_t.shape[0]
    tm = OUT_TM
    per_seq = seq // tm
    const2 = lambda i: (0, 0)
    return pl.pallas_call(
        _out_kernel,
        grid=(t // tm,),
        in_specs=[
            pl.BlockSpec((tm, d), lambda i: (i, 0)),
            pl.BlockSpec((tm, d), lambda i: (i, 0)),
            pl.BlockSpec((1, 6, d), lambda i: (i // per_seq, 0, 0)),
            pl.BlockSpec((d, d), const2),
            pl.BlockSpec((1, d), const2),
            pl.BlockSpec((1, d), const2),
            pl.BlockSpec((e, d), const2),
            pl.BlockSpec((e, d), const2),
        ],
        out_specs=[
            pl.BlockSpec((tm, d), lambda i: (i, 0)),
            pl.BlockSpec((tm, d // 2), lambda i: (i, 0)),
            pl.BlockSpec((e, tm), lambda i: (0, i)),
        ],
        out_shape=[
            jax.ShapeDtypeStruct((t, d), F32),
            jax.ShapeDtypeStruct((t, d // 2), U32),
            jax.ShapeDtypeStruct((e, t), F32),
        ],
        compiler_params=pltpu.CompilerParams(
            dimension_semantics=("arbitrary",), vmem_limit_bytes=VMEM_LIMIT_BYTES),
        name="out",
    )(cat, x2d, ada3, w_out_b, ln1_g.reshape(1, d), ln1_b.reshape(1, d), wr_hi_t, wr_lo_t)


def _route_kernel(lg_ref, bias_ref, idx_ref, gate_ref, rank_ref, cnt_ref, run_ref, gs_ref, keep_ref):
    i = pl.program_id(0)
    e, tr = lg_ref.shape
    per_group = e // N_EXPERT_GROUPS
    neg = -jnp.inf

    @pl.when(i == 0)
    def _():
        run_ref[...] = jnp.zeros_like(run_ref)

    scores = jax.nn.sigmoid(lg_ref[...])
    sel = scores + bias_ref[...]
    rowi = lax.broadcasted_iota(I32, (e, tr), 0)

    for g in range(N_EXPERT_GROUPS):
        rs = slice(g * per_group, (g + 1) * per_group)
        blk = sel[rs]
        ri = lax.broadcasted_iota(I32, (per_group, tr), 0) + g * per_group
        m1 = jnp.max(blk, axis=0, keepdims=True)
        i1 = jnp.min(jnp.where(blk == m1, ri, e), axis=0, keepdims=True)
        m2 = jnp.max(jnp.where(ri == i1, neg, blk), axis=0, keepdims=True)
        gs_ref[g:g + 1, :] = m1 + m2
    gmat = gs_ref[...]
    gi = lax.broadcasted_iota(I32, gmat.shape, 0)
    beaten = jnp.zeros(gmat.shape, I32)
    for g in range(N_EXPERT_GROUPS):
        gj = gs_ref[g:g + 1, :]
        beats = jnp.where(gj > gmat, 1, jnp.where((gj == gmat) & (gi > g), 1, 0))
        beaten = beaten + beats
    keep_ref[...] = jnp.where(beaten < TOPK_GROUPS, 1.0, 0.0)
    ekeep = jnp.concatenate(
        [jnp.broadcast_to(keep_ref[g:g + 1, :], (per_group, tr)) for g in range(N_EXPERT_GROUPS)], axis=0)
    masked = jnp.where(ekeep > 0.5, sel, neg)

    idxs, tops = [], []
    for k in range(TOP_K):
        m = jnp.max(masked, axis=0, keepdims=True)
        ik = jnp.min(jnp.where(masked == m, rowi, e), axis=0, keepdims=True)
        hit = rowi == ik
        tops.append(jnp.sum(jnp.where(hit, scores, 0.0), axis=0, keepdims=True))
        masked = jnp.where(hit, neg, masked)
        idxs.append(ik)
        idx_ref[k:k + 1, :] = ik
    total = tops[0]
    for k in range(1, TOP_K):
        total = total + tops[k]
    for k in range(TOP_K):
        gate_ref[k:k + 1, :] = tops[k] / total * ROUTED_SCALE

    chosen = jnp.zeros((e, tr), F32)
    for ik in idxs:
        chosen = chosen + jnp.where(rowi == ik, 1.0, 0.0)
    before = lax.broadcasted_iota(I32, (tr, tr), 0) < lax.broadcasted_iota(I32, (tr, tr), 1)
    upper = jnp.where(before, 1.0, 0.0).astype(BF16)
    pos = _dot(chosen.astype(BF16), upper) + run_ref[...]
    for k, ik in enumerate(idxs):
        rank_ref[k:k + 1, :] = jnp.sum(jnp.where(rowi == ik, pos, 0.0), axis=0, keepdims=True).astype(I32)
    run = run_ref[...] + jnp.sum(chosen, axis=1, keepdims=True)
    run_ref[...] = run
    cnt_ref[...] = jnp.broadcast_to(run, cnt_ref.shape).astype(I32)


def _route_call(logits_t, router_bias):
    e, t = logits_t.shape
    tr = ROUTE_TR
    tok = lambda i: (0, i)
    return pl.pallas_call(
        _route_kernel,
        grid=(t // tr,),
        in_specs=[pl.BlockSpec((e, tr), tok), pl.BlockSpec((e, 1), lambda i: (0, 0))],
        out_specs=[
            pl.BlockSpec((TOP_K, tr), tok),
            pl.BlockSpec((TOP_K, tr), tok),
            pl.BlockSpec((TOP_K, tr), tok),
            pl.BlockSpec((e, LANES), lambda i: (0, 0)),
        ],
        out_shape=[
            jax.ShapeDtypeStruct((TOP_K, t), I32),
            jax.ShapeDtypeStruct((TOP_K, t), F32),
            jax.ShapeDtypeStruct((TOP_K, t), I32),
            jax.ShapeDtypeStruct((e, LANES), I32),
        ],
        scratch_shapes=[pltpu.VMEM((e, 1), F32), pltpu.VMEM((N_EXPERT_GROUPS, tr), F32),
                        pltpu.VMEM((N_EXPERT_GROUPS, tr), F32)],
        compiler_params=pltpu.CompilerParams(
            dimension_semantics=("arbitrary",), vmem_limit_bytes=VMEM_LIMIT_BYTES),
        name="route",
    )(logits_t, router_bias.reshape(e, 1))


def _gmm_kernel(be_ref, nb_ref, tokc_ref, tokn_ref, gate_ref, h2p_hbm, w1_ref, w3_ref, w2_ref,
                o_ref, xbuf, sem, wb1, wb3, wb2):
    i = pl.program_id(0)
    tb, half = o_ref.shape
    nblk = nb_ref[0]
    slot = lax.rem(i, 2)

    def gather_copy(row, r, s):
        return pltpu.make_async_copy(h2p_hbm.at[pl.ds(row, 1)], xbuf.at[s, pl.ds(r, 1)], sem.at[s])

    def issue(tok_ref, s):
        def body(r, carry):
            gather_copy(tok_ref[0, 0, r], r, s).start()
            return carry
        lax.fori_loop(0, tb, body, 0)

    @pl.when(i == 0)
    def _():
        issue(tokc_ref, 0)

    @pl.when(i + 1 < nblk)
    def _():
        issue(tokn_ref, 1 - slot)

    active = i < nblk
    prev = be_ref[jnp.maximum(i - 1, 0)]
    fresh = jnp.logical_or(i == 0, be_ref[i] != prev)

    @pl.when(jnp.logical_and(active, fresh))
    def _():
        wb1[...] = w1_ref[0].astype(BF16)
        wb3[...] = w3_ref[0].astype(BF16)
        wb2[...] = w2_ref[0].astype(BF16)

    @pl.when(active)
    def _():
        pltpu.make_async_copy(h2p_hbm.at[pl.ds(0, tb)], xbuf.at[slot], sem.at[slot]).wait()
        xa, xb = _unpack_bf16_pair(xbuf[slot])
        xa = xa.astype(BF16)
        xb = xb.astype(BF16)
        h1 = _dot(xa, wb1[:half, :]) + _dot(xb, wb1[half:, :])
        h3 = _dot(xa, wb3[:half, :]) + _dot(xb, wb3[half:, :])
        act = (_silu(h1) * h3).astype(BF16)
        y = _dot(act, wb2[...]) * gate_ref[...]
        o_ref[...] = _pack_bf16_pair(y[:, :half], y[:, half:])

    @pl.when(jnp.logical_not(active))
    def _():
        o_ref[...] = jnp.zeros_like(o_ref)


def _gmm_call(block_e, nblk, buf_tok, buf_gate, h2p, w1, w3, w2):
    e, d, de = w1.shape
    half = d // 2
    tb = GMM_TB
    nb = block_e.shape[0]
    tok3 = buf_tok.reshape(nb, 1, tb)
    grid_spec = pltpu.PrefetchScalarGridSpec(
        num_scalar_prefetch=2,
        grid=(nb,),
        in_specs=[
            pl.BlockSpec((1, 1, tb), lambda i, be, n: (i, 0, 0), memory_space=pltpu.SMEM),
            pl.BlockSpec((1, 1, tb), lambda i, be, n: (jnp.minimum(i + 1, nb - 1), 0, 0),
                         memory_space=pltpu.SMEM),
            pl.BlockSpec((tb, 1), lambda i, be, n: (i, 0)),
            pl.BlockSpec(memory_space=pl.ANY),
            pl.BlockSpec((1, d, de), lambda i, be, n: (be[i], 0, 0)),
            pl.BlockSpec((1, d, de), lambda i, be, n: (be[i], 0, 0)),
            pl.BlockSpec((1, de, d), lambda i, be, n: (be[i], 0, 0)),
        ],
        out_specs=pl.BlockSpec((tb, half), lambda i, be, n: (i, 0)),
        scratch_shapes=[
            pltpu.VMEM((2, tb, half), U32),
            pltpu.SemaphoreType.DMA((2,)),
            pltpu.VMEM((d, de), BF16),
            pltpu.VMEM((d, de), BF16),
            pltpu.VMEM((de, d), BF16),
        ],
    )
    return pl.pallas_call(
        _gmm_kernel,
        grid_spec=grid_spec,
        out_shape=jax.ShapeDtypeStruct((nb * tb, half), U32),
        compiler_params=pltpu.CompilerParams(
            dimension_semantics=("arbitrary",), vmem_limit_bytes=VMEM_LIMIT_BYTES),
        name="gmm",
    )(block_e, nblk, tok3, tok3, buf_gate.reshape(nb * tb, 1), h2p, w1, w3, w2)


def _final_kernel(dest_ref, x1_ref, h2p_ref, ada_ref, ws1_ref, ws3_ref, ws2_ref, g_ref, b_ref, ys_hbm,
                  o_ref, gbuf, sem):
    tm, d = x1_ref.shape
    half = d // 2

    def row_copy(row, k, t):
        return pltpu.make_async_copy(ys_hbm.at[pl.ds(row, 1)], gbuf.at[k, pl.ds(t, 1)], sem.at[k])

    def issue(t, carry):
        for k in range(TOP_K):
            row_copy(dest_ref[k, t], k, t).start()
        return carry

    lax.fori_loop(0, tm, issue, 0)

    xa, xb = _unpack_bf16_pair(h2p_ref[...])
    xa = xa.astype(BF16)
    xb = xb.astype(BF16)
    h1 = _dot(xa, ws1_ref[:half, :]) + _dot(xb, ws1_ref[half:, :])
    h3 = _dot(xa, ws3_ref[:half, :]) + _dot(xb, ws3_ref[half:, :])
    shared = _dot((_silu(h1) * h3).astype(BF16), ws2_ref[...])

    ra = jnp.zeros((tm, half), F32)
    rb = jnp.zeros((tm, half), F32)
    for k in range(TOP_K):
        pltpu.make_async_copy(ys_hbm.at[pl.ds(0, tm)], gbuf.at[k], sem.at[k]).wait()
        ya, yb = _unpack_bf16_pair(gbuf[k])
        ra = ra + ya
        rb = rb + yb
    gate2 = ada_ref[0, 5:6, :]
    x1 = x1_ref[...]
    ya = DEEPNORM_ALPHA * x1[:, :half] + gate2[:, :half] * (shared[:, :half] + ra)
    yb = DEEPNORM_ALPHA * x1[:, half:] + gate2[:, half:] * (shared[:, half:] + rb)
    mu = (jnp.sum(ya, axis=-1, keepdims=True) + jnp.sum(yb, axis=-1, keepdims=True)) / d
    ca = ya - mu
    cb = yb - mu
    var = (jnp.sum(ca * ca, axis=-1, keepdims=True) + jnp.sum(cb * cb, axis=-1, keepdims=True)) / d
    inv = lax.rsqrt(var + LN_EPS)
    o_ref[:, :half] = ca * inv * g_ref[:, :half] + b_ref[:, :half]
    o_ref[:, half:] = cb * inv * g_ref[:, half:] + b_ref[:, half:]


def _final_call(dest, x1, h2p, ada3, ws1_b, ws3_b, ws2_b, ln2_g, ln2_b, ys, seq):
    t, d = x1.shape
    half = d // 2
    ds_ = ws1_b.shape[1]
    tm = FIN_TM
    per_seq = seq // tm
    const2 = lambda i: (0, 0)
    return pl.pallas_call(
        _final_kernel,
        grid=(t // tm,),
        in_specs=[
            pl.BlockSpec((TOP_K, tm), lambda i: (0, i), memory_space=pltpu.SMEM),
            pl.BlockSpec((tm, d), lambda i: (i, 0)),
            pl.BlockSpec((tm, half), lambda i: (i, 0)),
            pl.BlockSpec((1, 6, d), lambda i: (i // per_seq, 0, 0)),
            pl.BlockSpec((d, ds_), const2),
            pl.BlockSpec((d, ds_), const2),
            pl.BlockSpec((ds_, d), const2),
            pl.BlockSpec((1, d), const2),
            pl.BlockSpec((1, d), const2),
            pl.BlockSpec(memory_space=pl.ANY),
        ],
        out_specs=pl.BlockSpec((tm, d), lambda i: (i, 0)),
        out_shape=jax.ShapeDtypeStruct((t, d), F32),
        scratch_shapes=[pltpu.VMEM((TOP_K, tm, half), U32), pltpu.SemaphoreType.DMA((TOP_K,))],
        compiler_params=pltpu.CompilerParams(
            dimension_semantics=("arbitrary",), vmem_limit_bytes=VMEM_LIMIT_BYTES),
        name="final",
    )(dest, x1, h2p, ada3, ws1_b, ws3_b, ws2_b, ln2_g.reshape(1, d), ln2_b.reshape(1, d), ys)


def kernel(x, c, w_ada, b_ada, w_in, b_in, ln_v_g, ln_v_b, w_spatial, b_spatial, w_pool, b_pool, pool_scale, w_out, ln1_g, ln1_b, w_router, router_bias, w1, w3, w2, ws1, ws3, ws2, ln2_g, ln2_b):
    bsz, seq, d = x.shape
    t = bsz * seq
    e = w_router.shape[-1]
    assert w_ada.shape[0] == DEPTH
    l = 0

    ada3 = _ada_call(c, w_ada[l], b_ada[l]).reshape(bsz, 6, d)

    cat = _mix_call(x, ada3, w_in[l].astype(BF16), b_in[l], ln_v_g[l], ln_v_b[l], w_spatial[l],
                    b_spatial[l].T, w_pool[l].astype(BF16), b_pool[l], pool_scale[l])

    wr_t = w_router[l].T
    wr_hi = wr_t.astype(BF16)
    wr_lo = (wr_t - wr_hi.astype(F32)).astype(BF16)
    x1, h2p, logits_t = _out_call(cat.reshape(t, d), x.reshape(t, d), ada3, w_out[l].astype(BF16),
                                  ln1_g[l], ln1_b[l], wr_hi, wr_lo, seq)

    idx, gates, rank, cnt = _route_call(logits_t, router_bias[l])

    tb = GMM_TB
    nb = (t * TOP_K) // tb + e
    counts = cnt[:, 0]
    pcounts = (counts + tb - 1) // tb * tb
    pends = jnp.cumsum(pcounts)
    pstarts = pends - pcounts
    dest = pstarts[idx] + rank
    tok = jnp.broadcast_to(jnp.arange(t, dtype=I32)[None, :], dest.shape)
    buf_tok = jnp.zeros((nb * tb,), I32).at[dest.reshape(-1)].set(tok.reshape(-1))
    buf_gate = jnp.zeros((nb * tb,), F32).at[dest.reshape(-1)].set(gates.reshape(-1))
    block_e = jnp.searchsorted(pends, jnp.arange(nb, dtype=I32) * tb, side="right")
    block_e = jnp.minimum(block_e, e - 1).astype(I32)
    nblk = (pends[-1] // tb).astype(I32).reshape(1)

    ys = _gmm_call(block_e, nblk, buf_tok, buf_gate, h2p, w1.reshape(w1.shape[1:]),
                   w3.reshape(w3.shape[1:]), w2.reshape(w2.shape[1:]))

    out = _final_call(dest, x1, h2p, ada3, ws1[l].astype(BF16), ws3[l].astype(BF16), ws2[l].astype(BF16),
                      ln2_g[l], ln2_b[l], ys, seq)
    return out.reshape(bsz, seq, d)
```

```python
import functools

import jax
import jax.numpy as jnp
from jax import lax
from jax.experimental import pallas as pl
from jax.experimental.pallas import tpu as pltpu

F32 = jnp.float32
BF16 = jnp.bfloat16
U32 = jnp.uint32
I32 = jnp.int32

N_HEADS_A = 8
GMLP_BLOCK = 128
CHUNK = 64
POOL_WINDOWS = (2, 4, 8, 16)
POOL_HISTORY = 16
N_EXPERT_GROUPS = 8
TOPK_GROUPS = 4
TOP_K = 8
ROUTED_SCALE = 2.5
LN_EPS = 1e-5
DEPTH = 1
DEEPNORM_ALPHA = (2.0 * DEPTH) ** 0.25

LANES = 128
SUBLANES = 8
VMEM_LIMIT_BYTES = 56 * 1024 * 1024
ADA_TN = 1024
MIX_TS = 256
OUT_TM = 256
ROUTE_TR = 256
GMM_TB = 256
DISP_TM = 256
FIN_TM = 256
ROW_DMA_GROUP = 16
CAST_ROWS = 64


def _dot(a, b):
    return jnp.dot(a, b, preferred_element_type=F32)


def _split_bf16(a):
    hi = a.astype(BF16)
    lo = (a - hi.astype(F32)).astype(BF16)
    return hi, lo


def _gelu(x):
    return 0.5 * x * (1.0 + lax.erf(x * 0.7071067811865476))


def _silu(x):
    return x * jax.nn.sigmoid(x)


def _pack_bf16_pair(a, b):
    au = pltpu.bitcast(a.astype(BF16).astype(F32), U32)
    bu = pltpu.bitcast(b.astype(BF16).astype(F32), U32)
    return (au & jnp.uint32(0xFFFF0000)) | (bu >> 16)


def _unpack_bf16_pair(w):
    a = pltpu.bitcast(w & jnp.uint32(0xFFFF0000), F32)
    b = pltpu.bitcast(w << 16, F32)
    return a, b


def _ada_kernel(c_ref, w_ref, b_ref, o_ref):
    s = _silu(c_ref[...])
    sh, sl = _split_bf16(s)
    wh, wl = _split_bf16(w_ref[...])
    o_ref[...] = _dot(sh, wh) + _dot(sl, wh) + _dot(sh, wl) + b_ref[...]


def _ada_call(c, w_ada, b_ada):
    bsz, d = c.shape
    n = w_ada.shape[1]
    return pl.pallas_call(
        _ada_kernel,
        grid=(n // ADA_TN,),
        in_specs=[
            pl.BlockSpec((bsz, d), lambda j: (0, 0)),
            pl.BlockSpec((d, ADA_TN), lambda j: (0, j)),
            pl.BlockSpec((1, ADA_TN), lambda j: (0, j)),
        ],
        out_specs=pl.BlockSpec((bsz, ADA_TN), lambda j: (0, j)),
        out_shape=jax.ShapeDtypeStruct((bsz, n), F32),
        compiler_params=pltpu.CompilerParams(
            dimension_semantics=("arbitrary",), vmem_limit_bytes=VMEM_LIMIT_BYTES),
        name="ada",
    )(c, w_ada, b_ada.reshape(1, n))


def _mix_kernel(x_ref, ada_ref, win_ref, bin_ref, lng_ref, lnb_ref, wsp_ref, bsp_ref,
                wpool_ref, bpool_ref, pscale_ref, o_ref, carry_ref, *, d_a, d_g):
    j = pl.program_id(1)
    ts = x_ref.shape[1]
    x = x_ref[0]
    shift = ada_ref[0, 0:1, :]
    scale = ada_ref[0, 1:2, :]
    h = (x * (1.0 + scale) + shift).astype(BF16)
    proj = _dot(h, win_ref[...]) + bin_ref[...]

    u = _gelu(proj[:, :d_a])
    v = _gelu(proj[:, d_a:2 * d_a])
    mu = jnp.mean(v, axis=-1, keepdims=True)
    vc = v - mu
    var = jnp.mean(vc * vc, axis=-1, keepdims=True)
    vn = (vc * lax.rsqrt(var + LN_EPS) * lng_ref[...] + lnb_ref[...]).astype(BF16)
    row = lax.broadcasted_iota(I32, (GMLP_BLOCK, GMLP_BLOCK), 0)
    col = lax.broadcasted_iota(I32, (GMLP_BLOCK, GMLP_BLOCK), 1)
    causal = (col // CHUNK) <= (row // CHUNK)
    hd = d_a // N_HEADS_A
    for head in range(N_HEADS_A):
        w = jnp.where(causal, wsp_ref[head], 0.0).astype(BF16)
        bias = bsp_ref[:, head:head + 1]
        cs = slice(head * hd, (head + 1) * hd)
        for n in range(ts // GMLP_BLOCK):
            rs = slice(n * GMLP_BLOCK, (n + 1) * GMLP_BLOCK)
            mixed = _dot(w, vn[rs, cs]) + bias
            o_ref[0, rs, cs] = (u[rs, cs] * mixed).astype(BF16)

    z = proj[:, 2 * d_a:]

    @pl.when(j == 0)
    def _():
        carry_ref[...] = jnp.zeros_like(carry_ref)

    ext = jnp.concatenate([carry_ref[...], z], axis=0)
    carry_ref[...] = z[ts - POOL_HISTORY:, :]
    tpos = j * ts + lax.broadcasted_iota(I32, (ts, 1), 0)
    for g, win in enumerate(POOL_WINDOWS):
        gs = slice(g * d_g, (g + 1) * d_g)
        s = ext[:, gs]
        sh = 1
        while sh < win:
            s = s + pltpu.roll(s, sh, axis=0)
            sh *= 2
        cnt = jnp.minimum(tpos + 1, win).astype(F32)
        pooled = s[POOL_HISTORY:, :] / cnt - z[:, gs]
        y = _dot(pooled.astype(BF16), wpool_ref[g]) + bpool_ref[g:g + 1, :]
        o_ref[0, :, d_a + g * d_g:d_a + (g + 1) * d_g] = (y * pscale_ref[:, gs]).astype(BF16)


def _mix_call(x, ada3, w_in_b, b_in, ln_v_g, ln_v_b, w_spatial, b_spatial_t, w_pool_b, b_pool, pool_scale):
    bsz, seq, d = x.shape
    n_proj = w_in_b.shape[1]
    d_a = ln_v_g.shape[0]
    d_b = pool_scale.shape[0]
    n_g, d_g, _ = w_pool_b.shape
    ts = MIX_TS
    const2 = lambda b, j: (0, 0)
    const3 = lambda b, j: (0, 0, 0)
    return pl.pallas_call(
        functools.partial(_mix_kernel, d_a=d_a, d_g=d_g),
        grid=(bsz, seq // ts),
        in_specs=[
            pl.BlockSpec((1, ts, d), lambda b, j: (b, j, 0)),
            pl.BlockSpec((1, 6, d), lambda b, j: (b, 0, 0)),
            pl.BlockSpec((d, n_proj), const2),
            pl.BlockSpec((1, n_proj), const2),
            pl.BlockSpec((1, d_a), const2),
            pl.BlockSpec((1, d_a), const2),
            pl.BlockSpec((N_HEADS_A, GMLP_BLOCK, GMLP_BLOCK), const3),
            pl.BlockSpec((GMLP_BLOCK, N_HEADS_A), const2),
            pl.BlockSpec((n_g, d_g, d_g), const3),
            pl.BlockSpec((n_g, d_g), const2),
            pl.BlockSpec((1, d_b), const2),
        ],
        out_specs=pl.BlockSpec((1, ts, d_a + d_b), lambda b, j: (b, j, 0)),
        out_shape=jax.ShapeDtypeStruct((bsz, seq, d_a + d_b), BF16),
        scratch_shapes=[pltpu.VMEM((POOL_HISTORY, d_b), F32)],
        compiler_params=pltpu.CompilerParams(
            dimension_semantics=("arbitrary", "arbitrary"), vmem_limit_bytes=VMEM_LIMIT_BYTES),
        name="mix",
    )(x, ada3, w_in_b, b_in.reshape(1, n_proj), ln_v_g.reshape(1, d_a), ln_v_b.reshape(1, d_a),
      w_spatial, b_spatial_t, w_pool_b, b_pool, pool_scale.reshape(1, d_b))


def _layer_norm_rows(y, g, b):
    mu = jnp.mean(y, axis=-1, keepdims=True)
    yc = y - mu
    var = jnp.mean(yc * yc, axis=-1, keepdims=True)
    return yc * lax.rsqrt(var + LN_EPS) * g + b


def _out_kernel(cat_ref, x_ref, ada_ref, wout_ref, g_ref, b_ref, wrh_ref, wrl_ref,
                x1_ref, h2p_ref, lgt_ref):
    half = h2p_ref.shape[1]
    mix = _dot(cat_ref[...], wout_ref[...])
    gate1 = ada_ref[0, 2:3, :]
    x1 = _layer_norm_rows(DEEPNORM_ALPHA * x_ref[...] + gate1 * mix, g_ref[...], b_ref[...])
    x1_ref[...] = x1
    h2 = x1 * (1.0 + ada_ref[0, 4:5, :]) + ada_ref[0, 3:4, :]
    h2p_ref[...] = _pack_bf16_pair(h2[:, :half], h2[:, half:])
    hh, hl = _split_bf16(h2)
    nt = (((1,), (1,)), ((), ()))
    wrh = wrh_ref[...]
    lgt_ref[...] = (lax.dot_general(wrh, hh, nt, preferred_element_type=F32)
                    + lax.dot_general(wrh, hl, nt, preferred_element_type=F32)
                    + lax.dot_general(wrl_ref[...], hh, nt, preferred_element_type=F32))


def _out_call(cat, x2d, ada3, w_out_b, ln1_g, ln1_b, wr_hi_t, wr_lo_t, seq):
    t, d = x2d.shape
    e = wr_hi_t.shape[0]
    tm = OUT_TM
    per_seq = seq // tm
    const2 = lambda i: (0, 0)
    return pl.pallas_call(
        _out_kernel,
        grid=(t // tm,),
        in_specs=[
            pl.BlockSpec((tm, d), lambda i: (i, 0)),
            pl.BlockSpec((tm, d), lambda i: (i, 0)),
            pl.BlockSpec((1, 6, d), lambda i: (i // per_seq, 0, 0)),
            pl.BlockSpec((d, d), const2),
            pl.BlockSpec((1, d), const2),
            pl.BlockSpec((1, d), const2),
            pl.BlockSpec((e, d), const2),
            pl.BlockSpec((e, d), const2),
        ],
        out_specs=[
            pl.BlockSpec((tm, d), lambda i: (i, 0)),
            pl.BlockSpec((tm, d // 2), lambda i: (i, 0)),
            pl.BlockSpec((e, tm), lambda i: (0, i)),
        ],
        out_shape=[
            jax.ShapeDtypeStruct((t, d), F32),
            jax.ShapeDtypeStruct((t, d // 2), U32),
            jax.ShapeDtypeStruct((e, t), F32),
        ],
        compiler_params=pltpu.CompilerParams(
            dimension_semantics=("arbitrary",), vmem_limit_bytes=VMEM_LIMIT_BYTES),
        name="out",
    )(cat, x2d, ada3, w_out_b, ln1_g.reshape(1, d), ln1_b.reshape(1, d), wr_hi_t, wr_lo_t)


def _route_kernel(lg_ref, bias_ref, idx_ref, gate_ref, rank_ref, cnt_ref, run_ref, gs_ref, keep_ref):
    i = pl.program_id(0)
    e, tr = lg_ref.shape
    per_group = e // N_EXPERT_GROUPS
    neg = -jnp.inf

    @pl.when(i == 0)
    def _():
        run_ref[...] = jnp.zeros_like(run_ref)

    scores = jax.nn.sigmoid(lg_ref[...])
    sel = scores + bias_ref[...]
    rowi = lax.broadcasted_iota(I32, (e, tr), 0)

    for g in range(N_EXPERT_GROUPS):
        rs = slice(g * per_group, (g + 1) * per_group)
        blk = sel[rs]
        ri = lax.broadcasted_iota(I32, (per_group, tr), 0) + g * per_group
        m1 = jnp.max(blk, axis=0, keepdims=True)
        i1 = jnp.min(jnp.where(blk == m1, ri, e), axis=0, keepdims=True)
        m2 = jnp.max(jnp.where(ri == i1, neg, blk), axis=0, keepdims=True)
        gs_ref[g:g + 1, :] = m1 + m2
    gmat = gs_ref[...]
    gi = lax.broadcasted_iota(I32, gmat.shape, 0)
    beaten = jnp.zeros(gmat.shape, I32)
    for g in range(N_EXPERT_GROUPS):
        gj = gs_ref[g:g + 1, :]
        beats = jnp.where(gj > gmat, 1, jnp.where((gj == gmat) & (gi > g), 1, 0))
        beaten = beaten + beats
    keep_ref[...] = jnp.where(beaten < TOPK_GROUPS, 1.0, 0.0)
    ekeep = jnp.concatenate(
        [jnp.broadcast_to(keep_ref[g:g + 1, :], (per_group, tr)) for g in range(N_EXPERT_GROUPS)], axis=0)
    masked = jnp.where(ekeep > 0.5, sel, neg)

    idxs, tops = [], []
    for k in range(TOP_K):
        m = jnp.max(masked, axis=0, keepdims=True)
        ik = jnp.min(jnp.where(masked == m, rowi, e), axis=0, keepdims=True)
        hit = rowi == ik
        tops.append(jnp.sum(jnp.where(hit, scores, 0.0), axis=0, keepdims=True))
        masked = jnp.where(hit, neg, masked)
        idxs.append(ik)
        idx_ref[k:k + 1, :] = ik
    total = tops[0]
    for k in range(1, TOP_K):
        total = total + tops[k]
    for k in range(TOP_K):
        gate_ref[k:k + 1, :] = tops[k] / total * ROUTED_SCALE

    chosen = jnp.zeros((e, tr), F32)
    for ik in idxs:
        chosen = chosen + jnp.where(rowi == ik, 1.0, 0.0)
    before = lax.broadcasted_iota(I32, (tr, tr), 0) < lax.broadcasted_iota(I32, (tr, tr), 1)
    upper = jnp.where(before, 1.0, 0.0).astype(BF16)
    pos = _dot(chosen.astype(BF16), upper) + run_ref[...]
    for k, ik in enumerate(idxs):
        rank_ref[k:k + 1, :] = jnp.sum(jnp.where(rowi == ik, pos, 0.0), axis=0, keepdims=True).astype(I32)
    run = run_ref[...] + jnp.sum(chosen, axis=1, keepdims=True)
    run_ref[...] = run
    cnt_ref[...] = jnp.broadcast_to(run, cnt_ref.shape).astype(I32)


def _route_call(logits_t, router_bias):
    e, t = logits_t.shape
    tr = ROUTE_TR
    tok = lambda i: (0, i)
    return pl.pallas_call(
        _route_kernel,
        grid=(t // tr,),
        in_specs=[pl.BlockSpec((e, tr), tok), pl.BlockSpec((e, 1), lambda i: (0, 0))],
        out_specs=[
            pl.BlockSpec((TOP_K, tr), tok),
            pl.BlockSpec((TOP_K, tr), tok),
            pl.BlockSpec((TOP_K, tr), tok),
            pl.BlockSpec((e, LANES), lambda i: (0, 0)),
        ],
        out_shape=[
            jax.ShapeDtypeStruct((TOP_K, t), I32),
            jax.ShapeDtypeStruct((TOP_K, t), F32),
            jax.ShapeDtypeStruct((TOP_K, t), I32),
            jax.ShapeDtypeStruct((e, LANES), I32),
        ],
        scratch_shapes=[pltpu.VMEM((e, 1), F32), pltpu.VMEM((N_EXPERT_GROUPS, tr), F32),
                        pltpu.VMEM((N_EXPERT_GROUPS, tr), F32)],
        compiler_params=pltpu.CompilerParams(
            dimension_semantics=("arbitrary",), vmem_limit_bytes=VMEM_LIMIT_BYTES),
        name="route",
    )(logits_t, router_bias.reshape(e, 1))


def _plan_kernel(cnt_ref, pst_ref, last_ref, be_ref, nxt_ref, nblk_ref, *, tb):
    e = cnt_ref.shape[0]
    nb = be_ref.shape[1]
    shift = tb.bit_length() - 1
    nblocks = jnp.right_shift(cnt_ref[...] + (tb - 1), shift).astype(F32)
    lower = jnp.where(lax.broadcasted_iota(I32, (e, e), 1) <= lax.broadcasted_iota(I32, (e, e), 0), 1.0, 0.0)
    bend = _dot(lower.astype(BF16), nblocks.astype(BF16))
    nblk = bend[e - 1:e, :]
    pst_ref[...] = ((bend - nblocks) * tb).astype(I32)
    last_ref[...] = jnp.where(nblocks > 0, bend - 1.0, nblk).astype(I32)
    nblk_ref[...] = jnp.broadcast_to(nblk, nblk_ref.shape).astype(I32)
    blk = lax.broadcasted_iota(I32, (e, nb), 1).astype(F32)
    exp = lax.broadcasted_iota(I32, (e, nb), 0).astype(F32)
    be = jnp.minimum(jnp.sum(jnp.where(bend[:, 0:1] <= blk, 1.0, 0.0), axis=0, keepdims=True), e - 1.0)
    later = jnp.where((nblocks[:, 0:1] > 0) & (exp > be), exp, float(e))
    nxt = jnp.min(later, axis=0, keepdims=True)
    nxt = jnp.where(nxt >= e, -1.0, nxt)
    be_ref[...] = jnp.broadcast_to(be, be_ref.shape).astype(I32)
    nxt_ref[...] = jnp.broadcast_to(nxt, nxt_ref.shape).astype(I32)


def _plan_call(cnt, nb, tb):
    e = cnt.shape[0]
    return pl.pallas_call(
        functools.partial(_plan_kernel, tb=tb),
        out_shape=[
            jax.ShapeDtypeStruct((e, LANES), I32),
            jax.ShapeDtypeStruct((e, LANES), I32),
            jax.ShapeDtypeStruct((SUBLANES, nb), I32),
            jax.ShapeDtypeStruct((SUBLANES, nb), I32),
            jax.ShapeDtypeStruct((SUBLANES, LANES), I32),
        ],
        name="plan",
    )(cnt)


def _padzero_kernel(last_ref, o_ref):
    o_ref[...] = jnp.zeros_like(o_ref)


def _padzero_call(last_blk, nb, tb, half):
    e = last_blk.shape[0]
    grid_spec = pltpu.PrefetchScalarGridSpec(
        num_scalar_prefetch=1,
        grid=(e,),
        in_specs=[],
        out_specs=pl.BlockSpec((tb, half), lambda i, last: (last[i], 0)),
    )
    return pl.pallas_call(
        _padzero_kernel,
        grid_spec=grid_spec,
        out_shape=jax.ShapeDtypeStruct((nb * tb, half), U32),
        compiler_params=pltpu.CompilerParams(dimension_semantics=("arbitrary",)),
        name="padzero",
    )(last_blk)


def _dispatch_kernel(pst_ref, idx_ref, rank_ref, h_ref, xs_in, xs_out, sem):
    del xs_in
    tm = h_ref.shape[0]

    def row_copy(t, row):
        return pltpu.make_async_copy(h_ref.at[pl.ds(t, 1)], xs_out.at[pl.ds(row, 1)], sem)

    def group(g, carry):
        base = pl.multiple_of(g * ROW_DMA_GROUP, ROW_DMA_GROUP)
        for j in range(ROW_DMA_GROUP):
            for k in range(TOP_K):
                row = pst_ref[idx_ref[k, base + j]] + rank_ref[k, base + j]
                row_copy(base + j, row).start()
        return carry

    lax.fori_loop(0, tm // ROW_DMA_GROUP, group, 0)
    for k in range(TOP_K):
        pltpu.make_async_copy(h_ref, xs_out.at[pl.ds(0, tm)], sem).wait()


def _dispatch_call(pstarts, idx, rank, h2p, xs):
    t, half = h2p.shape
    tm = DISP_TM
    grid_spec = pltpu.PrefetchScalarGridSpec(
        num_scalar_prefetch=1,
        grid=(t // tm,),
        in_specs=[
            pl.BlockSpec((TOP_K, tm), lambda i, pst: (0, i), memory_space=pltpu.SMEM),
            pl.BlockSpec((TOP_K, tm), lambda i, pst: (0, i), memory_space=pltpu.SMEM),
            pl.BlockSpec((tm, half), lambda i, pst: (i, 0)),
            pl.BlockSpec(memory_space=pl.ANY),
        ],
        out_specs=pl.BlockSpec(memory_space=pl.ANY),
        scratch_shapes=[pltpu.SemaphoreType.DMA],
    )
    return pl.pallas_call(
        _dispatch_kernel,
        grid_spec=grid_spec,
        out_shape=jax.ShapeDtypeStruct(xs.shape, xs.dtype),
        input_output_aliases={4: 0},
        compiler_params=pltpu.CompilerParams(
            dimension_semantics=("arbitrary",), vmem_limit_bytes=VMEM_LIMIT_BYTES),
        name="dispatch",
    )(pstarts, idx, rank, h2p, xs)


def _gmm_kernel(be_ref, nxt_ref, nb_ref, x_ref, w1_hbm, w3_hbm, w2_hbm, o_ref,
                st1, st3, st2, sem, wb1, wb3, wb2):
    i = pl.program_id(0)
    half = x_ref.shape[1]
    nblk = nb_ref[0]

    def weight_copies(ex):
        return (pltpu.make_async_copy(w1_hbm.at[ex], st1, sem.at[0]),
                pltpu.make_async_copy(w3_hbm.at[ex], st3, sem.at[1]),
                pltpu.make_async_copy(w2_hbm.at[ex], st2, sem.at[2]))

    @pl.when(i == 0)
    def _():
        for cp in weight_copies(be_ref[0]):
            cp.start()

    active = i < nblk
    fresh = jnp.logical_or(i == 0, be_ref[i] != be_ref[jnp.maximum(i - 1, 0)])

    @pl.when(jnp.logical_and(active, fresh))
    def _():
        for cp in weight_copies(be_ref[i]):
            cp.wait()
        n_chunks = st1.shape[0] // CAST_ROWS
        rows2 = st2.shape[0] // n_chunks

        def cast_chunk(c, carry):
            r = pl.multiple_of(c * CAST_ROWS, CAST_ROWS)
            wb1[pl.ds(r, CAST_ROWS), :] = st1[pl.ds(r, CAST_ROWS), :].astype(BF16)
            wb3[pl.ds(r, CAST_ROWS), :] = st3[pl.ds(r, CAST_ROWS), :].astype(BF16)
            r2 = pl.multiple_of(c * rows2, rows2)
            wb2[pl.ds(r2, rows2), :] = st2[pl.ds(r2, rows2), :].astype(BF16)
            return carry

        lax.fori_loop(0, n_chunks, cast_chunk, 0)
        nxt = nxt_ref[i]

        @pl.when(nxt >= 0)
        def _():
            for cp in weight_copies(nxt):
                cp.start()

    @pl.when(active)
    def _():
        xa, xb = _unpack_bf16_pair(x_ref[...])
        xa = xa.astype(BF16)
        xb = xb.astype(BF16)
        h1 = _dot(xa, wb1[:half, :]) + _dot(xb, wb1[half:, :])
        h3 = _dot(xa, wb3[:half, :]) + _dot(xb, wb3[half:, :])
        y = _dot((_silu(h1) * h3).astype(BF16), wb2[...])
        o_ref[...] = _pack_bf16_pair(y[:, :half], y[:, half:])

    @pl.when(jnp.logical_not(active))
    def _():
        o_ref[...] = jnp.zeros_like(o_ref)


def _gmm_call(block_e, next_e, nblk, xs, w1, w3, w2):
    e, d, de = w1.shape
    half = d // 2
    tb = GMM_TB
    nb = block_e.shape[0]
    grid_spec = pltpu.PrefetchScalarGridSpec(
        num_scalar_prefetch=3,
        grid=(nb,),
        in_specs=[
            pl.BlockSpec((tb, half), lambda i, be, nx, n: (jnp.minimum(i, n[0] - 1), 0)),
            pl.BlockSpec(memory_space=pl.ANY),
            pl.BlockSpec(memory_space=pl.ANY),
            pl.BlockSpec(memory_space=pl.ANY),
        ],
        out_specs=pl.BlockSpec((tb, half), lambda i, be, nx, n: (jnp.minimum(i, n[0]), 0)),
        scratch_shapes=[
            pltpu.VMEM((d, de), F32),
            pltpu.VMEM((d, de), F32),
            pltpu.VMEM((de, d), F32),
            pltpu.SemaphoreType.DMA((3,)),
            pltpu.VMEM((d, de), BF16),
            pltpu.VMEM((d, de), BF16),
            pltpu.VMEM((de, d), BF16),
        ],
    )
    return pl.pallas_call(
        _gmm_kernel,
        grid_spec=grid_spec,
        out_shape=jax.ShapeDtypeStruct((nb * tb, half), U32),
        compiler_params=pltpu.CompilerParams(
            dimension_semantics=("arbitrary",), vmem_limit_bytes=VMEM_LIMIT_BYTES),
        name="gmm",
    )(block_e, next_e, nblk, xs, w1, w3, w2)


def _final_kernel(pst_ref, idx_ref, rank_ref, gt_ref, x1_ref, h2p_ref, ada_ref, ws1_ref, ws3_ref, ws2_ref,
                  g_ref, b_ref, ys_hbm, o_ref, gbuf, sem):
    tm, d = x1_ref.shape
    half = d // 2

    def row_copy(row, k, t):
        return pltpu.make_async_copy(ys_hbm.at[pl.ds(row, 1)], gbuf.at[k, pl.ds(t, 1)], sem.at[k])

    def group(g, carry):
        base = pl.multiple_of(g * ROW_DMA_GROUP, ROW_DMA_GROUP)
        for j in range(ROW_DMA_GROUP):
            for k in range(TOP_K):
                row = pst_ref[idx_ref[k, base + j]] + rank_ref[k, base + j]
                row_copy(row, k, base + j).start()
        return carry

    lax.fori_loop(0, tm // ROW_DMA_GROUP, group, 0)

    xa, xb = _unpack_bf16_pair(h2p_ref[...])
    xa = xa.astype(BF16)
    xb = xb.astype(BF16)
    h1 = _dot(xa, ws1_ref[:half, :]) + _dot(xb, ws1_ref[half:, :])
    h3 = _dot(xa, ws3_ref[:half, :]) + _dot(xb, ws3_ref[half:, :])
    shared = _dot((_silu(h1) * h3).astype(BF16), ws2_ref[...])

    ra = jnp.zeros((tm, half), F32)
    rb = jnp.zeros((tm, half), F32)
    for k in range(TOP_K):
        pltpu.make_async_copy(ys_hbm.at[pl.ds(0, tm)], gbuf.at[k], sem.at[k]).wait()
        ya, yb = _unpack_bf16_pair(gbuf[k])
        gk = gt_ref[:, k:k + 1]
        ra = ra + ya * gk
        rb = rb + yb * gk
    gate2 = ada_ref[0, 5:6, :]
    x1 = x1_ref[...]
    ya = DEEPNORM_ALPHA * x1[:, :half] + gate2[:, :half] * (shared[:, :half] + ra)
    yb = DEEPNORM_ALPHA * x1[:, half:] + gate2[:, half:] * (shared[:, half:] + rb)
    mu = (jnp.sum(ya, axis=-1, keepdims=True) + jnp.sum(yb, axis=-1, keepdims=True)) / d
    ca = ya - mu
    cb = yb - mu
    var = (jnp.sum(ca * ca, axis=-1, keepdims=True) + jnp.sum(cb * cb, axis=-1, keepdims=True)) / d
    inv = lax.rsqrt(var + LN_EPS)
    o_ref[:, :half] = ca * inv * g_ref[:, :half] + b_ref[:, :half]
    o_ref[:, half:] = cb * inv * g_ref[:, half:] + b_ref[:, half:]


def _final_call(pstarts, idx, rank, gates_t, x1, h2p, ada3, ws1_b, ws3_b, ws2_b, ln2_g, ln2_b, ys, seq):
    t, d = x1.shape
    half = d // 2
    ds_ = ws1_b.shape[1]
    tm = FIN_TM
    per_seq = seq // tm
    const2 = lambda i, pst: (0, 0)
    grid_spec = pltpu.PrefetchScalarGridSpec(
        num_scalar_prefetch=1,
        grid=(t // tm,),
        in_specs=[
            pl.BlockSpec((TOP_K, tm), lambda i, pst: (0, i), memory_space=pltpu.SMEM),
            pl.BlockSpec((TOP_K, tm), lambda i, pst: (0, i), memory_space=pltpu.SMEM),
            pl.BlockSpec((tm, TOP_K), lambda i, pst: (i, 0)),
            pl.BlockSpec((tm, d), lambda i, pst: (i, 0)),
            pl.BlockSpec((tm, half), lambda i, pst: (i, 0)),
            pl.BlockSpec((1, 6, d), lambda i, pst: (i // per_seq, 0, 0)),
            pl.BlockSpec((d, ds_), const2),
            pl.BlockSpec((d, ds_), const2),
            pl.BlockSpec((ds_, d), const2),
            pl.BlockSpec((1, d), const2),
            pl.BlockSpec((1, d), const2),
            pl.BlockSpec(memory_space=pl.ANY),
        ],
        out_specs=pl.BlockSpec((tm, d), lambda i, pst: (i, 0)),
        scratch_shapes=[pltpu.VMEM((TOP_K, tm, half), U32), pltpu.SemaphoreType.DMA((TOP_K,))],
    )
    return pl.pallas_call(
        _final_kernel,
        grid_spec=grid_spec,
        out_shape=jax.ShapeDtypeStruct((t, d), F32),
        compiler_params=pltpu.CompilerParams(
            dimension_semantics=("arbitrary",), vmem_limit_bytes=VMEM_LIMIT_BYTES),
        name="final",
    )(pstarts, idx, rank, gates_t, x1, h2p, ada3, ws1_b, ws3_b, ws2_b,
      ln2_g.reshape(1, d), ln2_b.reshape(1, d), ys)


def kernel(x, c, w_ada, b_ada, w_in, b_in, ln_v_g, ln_v_b, w_spatial, b_spatial, w_pool, b_pool, pool_scale, w_out, ln1_g, ln1_b, w_router, router_bias, w1, w3, w2, ws1, ws3, ws2, ln2_g, ln2_b):
    bsz, seq, d = x.shape
    t = bsz * seq
    e = w_router.shape[-1]
    assert w_ada.shape[0] == DEPTH
    l = 0

    ada3 = _ada_call(c, w_ada[l], b_ada[l]).reshape(bsz, 6, d)

    cat = _mix_call(x, ada3, w_in[l].astype(BF16), b_in[l], ln_v_g[l], ln_v_b[l], w_spatial[l],
                    b_spatial[l].T, w_pool[l].astype(BF16), b_pool[l], pool_scale[l])

    wr_t = w_router[l].T
    wr_hi = wr_t.astype(BF16)
    wr_lo = (wr_t - wr_hi.astype(F32)).astype(BF16)
    x1, h2p, logits_t = _out_call(cat.reshape(t, d), x.reshape(t, d), ada3, w_out[l].astype(BF16),
                                  ln1_g[l], ln1_b[l], wr_hi, wr_lo, seq)

    idx, gates, rank, cnt = _route_call(logits_t, router_bias[l])

    tb = GMM_TB
    nb = (t * TOP_K) // tb + e
    pst, last, be, nxt, nblk = _plan_call(cnt, nb, tb)
    pstarts, last_blk, block_e, next_e, nblk = pst[:, 0], last[:, 0], be[0], nxt[0], nblk[0, :1]

    xs = _dispatch_call(pstarts, idx, rank, h2p, _padzero_call(last_blk, nb, tb, d // 2))

    ys = _gmm_call(block_e, next_e, nblk, xs, w1.reshape(w1.shape[1:]),
                   w3.reshape(w3.shape[1:]), w2.reshape(w2.shape[1:]))

    out = _final_call(pstarts, idx, rank, gates.T, x1, h2p, ada3, ws1[l].astype(BF16), ws3[l].astype(BF16),
                      ws2[l].astype(BF16), ln2_g[l], ln2_b[l], ys, seq)
    return out.reshape(bsz, seq, d)
```

```python
import functools

import jax
import jax.numpy as jnp
from jax import lax
from jax.experimental import pallas as pl
from jax.experimental.pallas import tpu as pltpu

F32 = jnp.float32
BF16 = jnp.bfloat16
U32 = jnp.uint32
I32 = jnp.int32

N_HEADS_A = 8
GMLP_BLOCK = 128
CHUNK = 64
POOL_WINDOWS = (2, 4, 8, 16)
POOL_HISTORY = 16
N_EXPERT_GROUPS = 8
TOPK_GROUPS = 4
TOP_K = 8
ROUTED_SCALE = 2.5
LN_EPS = 1e-5
DEPTH = 1
DEEPNORM_ALPHA = (2.0 * DEPTH) ** 0.25

LANES = 128
SUBLANES = 8
VMEM_LIMIT_BYTES = 56 * 1024 * 1024
ADA_TN = 1024
MIX_TS = 256
OUT_TM = 256
ROUTE_TR = 256
PLAN_TR = 512
GMM_TB = 256
DISP_TM = 256
FIN_TM = 256
ROW_DMA_GROUP = 16
CAST_ROWS = 64


def _dot(a, b):
    return jnp.dot(a, b, preferred_element_type=F32)


def _split_bf16(a):
    hi = a.astype(BF16)
    lo = (a - hi.astype(F32)).astype(BF16)
    return hi, lo


def _gelu(x):
    return 0.5 * x * (1.0 + lax.erf(x * 0.7071067811865476))


def _silu(x):
    return x * jax.nn.sigmoid(x)


def _pack_bf16_pair(a, b):
    au = pltpu.bitcast(a.astype(BF16).astype(F32), U32)
    bu = pltpu.bitcast(b.astype(BF16).astype(F32), U32)
    return (au & jnp.uint32(0xFFFF0000)) | (bu >> 16)


def _unpack_bf16_pair(w):
    a = pltpu.bitcast(w & jnp.uint32(0xFFFF0000), F32)
    b = pltpu.bitcast(w << 16, F32)
    return a, b


def _tiles_to_rows(ref, n):
    return jnp.concatenate([ref[pl.ds(c, n, stride=SUBLANES), :] for c in range(SUBLANES)], axis=1)


def _rows_to_tiles(ref, value):
    n = value.shape[0]
    for c in range(SUBLANES):
        ref[pl.ds(c, n, stride=SUBLANES), :] = value[:, c * LANES:(c + 1) * LANES]


def _ada_kernel(c_ref, w_ref, b_ref, o_ref):
    s = _silu(c_ref[...])
    sh, sl = _split_bf16(s)
    wh, wl = _split_bf16(w_ref[...])
    o_ref[...] = _dot(sh, wh) + _dot(sl, wh) + _dot(sh, wl) + b_ref[...]


def _ada_call(c, w_ada, b_ada):
    bsz, d = c.shape
    n = w_ada.shape[1]
    return pl.pallas_call(
        _ada_kernel,
        grid=(n // ADA_TN,),
        in_specs=[
            pl.BlockSpec((bsz, d), lambda j: (0, 0)),
            pl.BlockSpec((d, ADA_TN), lambda j: (0, j)),
            pl.BlockSpec((1, ADA_TN), lambda j: (0, j)),
        ],
        out_specs=pl.BlockSpec((bsz, ADA_TN), lambda j: (0, j)),
        out_shape=jax.ShapeDtypeStruct((bsz, n), F32),
        compiler_params=pltpu.CompilerParams(
            dimension_semantics=("arbitrary",), vmem_limit_bytes=VMEM_LIMIT_BYTES),
        name="ada",
    )(c, w_ada, b_ada.reshape(1, n))


def _mix_kernel(x_ref, ada_ref, win_ref, bin_ref, lng_ref, lnb_ref, wsp_ref, bsp_ref,
                wpool_ref, bpool_ref, pscale_ref, o_ref, carry_ref, *, d_a, d_g):
    j = pl.program_id(1)
    ts = x_ref.shape[1]
    x = x_ref[0]
    shift = ada_ref[0, 0:1, :]
    scale = ada_ref[0, 1:2, :]
    h = (x * (1.0 + scale) + shift).astype(BF16)
    proj = _dot(h, win_ref[...]) + bin_ref[...]

    u = _gelu(proj[:, :d_a])
    v = _gelu(proj[:, d_a:2 * d_a])
    mu = jnp.mean(v, axis=-1, keepdims=True)
    vc = v - mu
    var = jnp.mean(vc * vc, axis=-1, keepdims=True)
    vn = (vc * lax.rsqrt(var + LN_EPS) * lng_ref[...] + lnb_ref[...]).astype(BF16)
    row = lax.broadcasted_iota(I32, (GMLP_BLOCK, GMLP_BLOCK), 0)
    col = lax.broadcasted_iota(I32, (GMLP_BLOCK, GMLP_BLOCK), 1)
    causal = (col // CHUNK) <= (row // CHUNK)
    hd = d_a // N_HEADS_A
    for head in range(N_HEADS_A):
        w = jnp.where(causal, wsp_ref[head], 0.0).astype(BF16)
        bias = bsp_ref[:, head:head + 1]
        cs = slice(head * hd, (head + 1) * hd)
        for n in range(ts // GMLP_BLOCK):
            rs = slice(n * GMLP_BLOCK, (n + 1) * GMLP_BLOCK)
            mixed = _dot(w, vn[rs, cs]) + bias
            o_ref[0, rs, cs] = (u[rs, cs] * mixed).astype(BF16)

    z = proj[:, 2 * d_a:]

    @pl.when(j == 0)
    def _():
        carry_ref[...] = jnp.zeros_like(carry_ref)

    ext = jnp.concatenate([carry_ref[...], z], axis=0)
    carry_ref[...] = z[ts - POOL_HISTORY:, :]
    tpos = j * ts + lax.broadcasted_iota(I32, (ts, 1), 0)
    for g, win in enumerate(POOL_WINDOWS):
        gs = slice(g * d_g, (g + 1) * d_g)
        s = ext[:, gs]
        sh = 1
        while sh < win:
            s = s + pltpu.roll(s, sh, axis=0)
            sh *= 2
        cnt = jnp.minimum(tpos + 1, win).astype(F32)
        pooled = s[POOL_HISTORY:, :] / cnt - z[:, gs]
        y = _dot(pooled.astype(BF16), wpool_ref[g]) + bpool_ref[g:g + 1, :]
        o_ref[0, :, d_a + g * d_g:d_a + (g + 1) * d_g] = (y * pscale_ref[:, gs]).astype(BF16)


def _mix_call(x, ada3, w_in_b, b_in, ln_v_g, ln_v_b, w_spatial, b_spatial_t, w_pool_b, b_pool, pool_scale):
    bsz, seq, d = x.shape
    n_proj = w_in_b.shape[1]
    d_a = ln_v_g.shape[0]
    d_b = pool_scale.shape[0]
    n_g, d_g, _ = w_pool_b.shape
    ts = MIX_TS
    const2 = lambda b, j: (0, 0)
    const3 = lambda b, j: (0, 0, 0)
    return pl.pallas_call(
        functools.partial(_mix_kernel, d_a=d_a, d_g=d_g),
        grid=(bsz, seq // ts),
        in_specs=[
            pl.BlockSpec((1, ts, d), lambda b, j: (b, j, 0)),
            pl.BlockSpec((1, 6, d), lambda b, j: (b, 0, 0)),
            pl.BlockSpec((d, n_proj), const2),
            pl.BlockSpec((1, n_proj), const2),
            pl.BlockSpec((1, d_a), const2),
            pl.BlockSpec((1, d_a), const2),
            pl.BlockSpec((N_HEADS_A, GMLP_BLOCK, GMLP_BLOCK), const3),
            pl.BlockSpec((GMLP_BLOCK, N_HEADS_A), const2),
            pl.BlockSpec((n_g, d_g, d_g), const3),
            pl.BlockSpec((n_g, d_g), const2),
            pl.BlockSpec((1, d_b), const2),
        ],
        out_specs=pl.BlockSpec((1, ts, d_a + d_b), lambda b, j: (b, j, 0)),
        out_shape=jax.ShapeDtypeStruct((bsz, seq, d_a + d_b), BF16),
        scratch_shapes=[pltpu.VMEM((POOL_HISTORY, d_b), F32)],
        compiler_params=pltpu.CompilerParams(
            dimension_semantics=("arbitrary", "arbitrary"), vmem_limit_bytes=VMEM_LIMIT_BYTES),
        name="mix",
    )(x, ada3, w_in_b, b_in.reshape(1, n_proj), ln_v_g.reshape(1, d_a), ln_v_b.reshape(1, d_a),
      w_spatial, b_spatial_t, w_pool_b, b_pool, pool_scale.reshape(1, d_b))


def _layer_norm_rows(y, g, b):
    mu = jnp.mean(y, axis=-1, keepdims=True)
    yc = y - mu
    var = jnp.mean(yc * yc, axis=-1, keepdims=True)
    return yc * lax.rsqrt(var + LN_EPS) * g + b


def _out_kernel(cat_ref, x_ref, ada_ref, wout_ref, g_ref, b_ref, wrh_ref, wrl_ref,
                x1_ref, h2p_ref, lgt_ref):
    half = x_ref.shape[1] // 2
    mix = _dot(cat_ref[...], wout_ref[...])
    gate1 = ada_ref[0, 2:3, :]
    x1 = _layer_norm_rows(DEEPNORM_ALPHA * x_ref[...] + gate1 * mix, g_ref[...], b_ref[...])
    x1_ref[...] = x1
    h2 = x1 * (1.0 + ada_ref[0, 4:5, :]) + ada_ref[0, 3:4, :]
    _rows_to_tiles(h2p_ref, _pack_bf16_pair(h2[:, :half], h2[:, half:]))
    hh, hl = _split_bf16(h2)
    nt = (((1,), (1,)), ((), ()))
    wrh = wrh_ref[...]
    lgt_ref[...] = (lax.dot_general(wrh, hh, nt, preferred_element_type=F32)
                    + lax.dot_general(wrh, hl, nt, preferred_element_type=F32)
                    + lax.dot_general(wrl_ref[...], hh, nt, preferred_element_type=F32))


def _out_call(cat, x2d, ada3, w_out_b, ln1_g, ln1_b, wr_hi_t, wr_lo_t, seq):
    t, d = x2d.shape
    e = wr_hi_t.shape[0]
    tm = OUT_TM
    per_seq = seq // tm
    const2 = lambda i: (0, 0)
    return pl.pallas_call(
        _out_kernel,
        grid=(t // tm,),
        in_specs=[
            pl.BlockSpec((tm, d), lambda i: (i, 0)),
            pl.BlockSpec((tm, d), lambda i: (i, 0)),
            pl.BlockSpec((1, 6, d), lambda i: (i // per_seq, 0, 0)),
            pl.BlockSpec((d, d), const2),
            pl.BlockSpec((1, d), const2),
            pl.BlockSpec((1, d), const2),
            pl.BlockSpec((e, d), const2),
            pl.BlockSpec((e, d), const2),
        ],
        out_specs=[
            pl.BlockSpec((tm, d), lambda i: (i, 0)),
            pl.BlockSpec((tm * SUBLANES, LANES), lambda i: (i, 0)),
            pl.BlockSpec((e, tm), lambda i: (0, i)),
        ],
        out_shape=[
            jax.ShapeDtypeStruct((t, d), F32),
            jax.ShapeDtypeStruct((t * SUBLANES, LANES), U32),
            jax.ShapeDtypeStruct((e, t), F32),
        ],
        compiler_params=pltpu.CompilerParams(
            dimension_semantics=("arbitrary",), vmem_limit_bytes=VMEM_LIMIT_BYTES),
        name="out",
    )(cat, x2d, ada3, w_out_b, ln1_g.reshape(1, d), ln1_b.reshape(1, d), wr_hi_t, wr_lo_t)


def _route_kernel(lg_ref, bias_ref, idx_ref, gate_ref, rank_ref, cnt_ref, run_ref, gs_ref, keep_ref):
    i = pl.program_id(0)
    e, tr = lg_ref.shape
    per_group = e // N_EXPERT_GROUPS
    neg = -jnp.inf

    @pl.when(i == 0)
    def _():
        run_ref[...] = jnp.zeros_like(run_ref)

    scores = jax.nn.sigmoid(lg_ref[...])
    sel = scores + bias_ref[...]
    rowi = lax.broadcasted_iota(I32, (e, tr), 0)

    for g in range(N_EXPERT_GROUPS):
        rs = slice(g * per_group, (g + 1) * per_group)
        blk = sel[rs]
        ri = lax.broadcasted_iota(I32, (per_group, tr), 0) + g * per_group
        m1 = jnp.max(blk, axis=0, keepdims=True)
        i1 = jnp.min(jnp.where(blk == m1, ri, e), axis=0, keepdims=True)
        m2 = jnp.max(jnp.where(ri == i1, neg, blk), axis=0, keepdims=True)
        gs_ref[g:g + 1, :] = m1 + m2
    gmat = gs_ref[...]
    gi = lax.broadcasted_iota(I32, gmat.shape, 0)
    beaten = jnp.zeros(gmat.shape, I32)
    for g in range(N_EXPERT_GROUPS):
        gj = gs_ref[g:g + 1, :]
        beats = jnp.where(gj > gmat, 1, jnp.where((gj == gmat) & (gi > g), 1, 0))
        beaten = beaten + beats
    keep_ref[...] = jnp.where(beaten < TOPK_GROUPS, 1.0, 0.0)
    ekeep = jnp.concatenate(
        [jnp.broadcast_to(keep_ref[g:g + 1, :], (per_group, tr)) for g in range(N_EXPERT_GROUPS)], axis=0)
    masked = jnp.where(ekeep > 0.5, sel, neg)

    idxs, tops = [], []
    for k in range(TOP_K):
        m = jnp.max(masked, axis=0, keepdims=True)
        ik = jnp.min(jnp.where(masked == m, rowi, e), axis=0, keepdims=True)
        hit = rowi == ik
        tops.append(jnp.sum(jnp.where(hit, scores, 0.0), axis=0, keepdims=True))
        masked = jnp.where(hit, neg, masked)
        idxs.append(ik)
        idx_ref[k:k + 1, :] = ik
    total = tops[0]
    for k in range(1, TOP_K):
        total = total + tops[k]
    for k in range(TOP_K):
        gate_ref[k:k + 1, :] = tops[k] / total * ROUTED_SCALE

    chosen = jnp.zeros((e, tr), F32)
    for ik in idxs:
        chosen = chosen + jnp.where(rowi == ik, 1.0, 0.0)
    before = lax.broadcasted_iota(I32, (tr, tr), 0) < lax.broadcasted_iota(I32, (tr, tr), 1)
    upper = jnp.where(before, 1.0, 0.0).astype(BF16)
    pos = _dot(chosen.astype(BF16), upper) + run_ref[...]
    for k, ik in enumerate(idxs):
        rank_ref[k:k + 1, :] = jnp.sum(jnp.where(rowi == ik, pos, 0.0), axis=0, keepdims=True).astype(I32)
    run = run_ref[...] + jnp.sum(chosen, axis=1, keepdims=True)
    run_ref[...] = run
    cnt_ref[...] = jnp.broadcast_to(run, cnt_ref.shape).astype(I32)


def _route_call(logits_t, router_bias):
    e, t = logits_t.shape
    tr = ROUTE_TR
    tok = lambda i: (0, i)
    return pl.pallas_call(
        _route_kernel,
        grid=(t // tr,),
        in_specs=[pl.BlockSpec((e, tr), tok), pl.BlockSpec((e, 1), lambda i: (0, 0))],
        out_specs=[
            pl.BlockSpec((TOP_K, tr), tok),
            pl.BlockSpec((TOP_K, tr), tok),
            pl.BlockSpec((TOP_K, tr), tok),
            pl.BlockSpec((e, LANES), lambda i: (0, 0)),
        ],
        out_shape=[
            jax.ShapeDtypeStruct((TOP_K, t), I32),
            jax.ShapeDtypeStruct((TOP_K, t), F32),
            jax.ShapeDtypeStruct((TOP_K, t), I32),
            jax.ShapeDtypeStruct((e, LANES), I32),
        ],
        scratch_shapes=[pltpu.VMEM((e, 1), F32), pltpu.VMEM((N_EXPERT_GROUPS, tr), F32),
                        pltpu.VMEM((N_EXPERT_GROUPS, tr), F32)],
        compiler_params=pltpu.CompilerParams(
            dimension_semantics=("arbitrary",), vmem_limit_bytes=VMEM_LIMIT_BYTES),
        name="route",
    )(logits_t, router_bias.reshape(e, 1))


def _plan_kernel(cnt_ref, idx_ref, rank_ref, dest_ref, last_ref, be_ref, nxt_ref, nblk_ref, pst_ref, *, tb):
    e = cnt_ref.shape[0]
    nb = be_ref.shape[1]
    tr = idx_ref.shape[1]

    @pl.when(pl.program_id(0) == 0)
    def _():
        shift = tb.bit_length() - 1
        nblocks = jnp.right_shift(cnt_ref[...] + (tb - 1), shift).astype(F32)
        lower = jnp.where(lax.broadcasted_iota(I32, (e, e), 1) <= lax.broadcasted_iota(I32, (e, e), 0), 1.0, 0.0)
        bend = _dot(lower.astype(BF16), nblocks.astype(BF16))
        nblk = bend[e - 1:e, :]
        pst_ref[...] = ((bend - nblocks) * tb)[:, 0:1]
        last_ref[...] = jnp.where(nblocks > 0, bend - 1.0, nblk).astype(I32)
        nblk_ref[...] = jnp.broadcast_to(nblk, nblk_ref.shape).astype(I32)
        blk = lax.broadcasted_iota(I32, (e, nb), 1).astype(F32)
        exp = lax.broadcasted_iota(I32, (e, nb), 0).astype(F32)
        be = jnp.minimum(jnp.sum(jnp.where(bend[:, 0:1] <= blk, 1.0, 0.0), axis=0, keepdims=True), e - 1.0)
        later = jnp.where((nblocks[:, 0:1] > 0) & (exp > be), exp, float(e))
        nxt = jnp.min(later, axis=0, keepdims=True)
        nxt = jnp.where(nxt >= e, -1.0, nxt)
        be_ref[...] = jnp.broadcast_to(be, be_ref.shape).astype(I32)
        nxt_ref[...] = jnp.broadcast_to(nxt, nxt_ref.shape).astype(I32)

    rowi = lax.broadcasted_iota(I32, (e, tr), 0)
    pst = pst_ref[...]
    for k in range(TOP_K):
        start = jnp.sum(jnp.where(rowi == idx_ref[k:k + 1, :], pst, 0.0), axis=0, keepdims=True)
        dest_ref[k:k + 1, :] = start.astype(I32) + rank_ref[k:k + 1, :]


def _plan_call(cnt, idx, rank, nb, tb):
    e = cnt.shape[0]
    t = idx.shape[1]
    tr = PLAN_TR
    tok = lambda i: (0, i)
    const = lambda i: (0, 0)
    return pl.pallas_call(
        functools.partial(_plan_kernel, tb=tb),
        grid=(t // tr,),
        in_specs=[pl.BlockSpec((e, LANES), const), pl.BlockSpec((TOP_K, tr), tok), pl.BlockSpec((TOP_K, tr), tok)],
        out_specs=[
            pl.BlockSpec((TOP_K, tr), tok),
            pl.BlockSpec((e, LANES), const),
            pl.BlockSpec((SUBLANES, nb), const),
            pl.BlockSpec((SUBLANES, nb), const),
            pl.BlockSpec((SUBLANES, LANES), const),
        ],
        out_shape=[
            jax.ShapeDtypeStruct((TOP_K, t), I32),
            jax.ShapeDtypeStruct((e, LANES), I32),
            jax.ShapeDtypeStruct((SUBLANES, nb), I32),
            jax.ShapeDtypeStruct((SUBLANES, nb), I32),
            jax.ShapeDtypeStruct((SUBLANES, LANES), I32),
        ],
        scratch_shapes=[pltpu.VMEM((e, 1), F32)],
        compiler_params=pltpu.CompilerParams(dimension_semantics=("arbitrary",)),
        name="plan",
    )(cnt, idx, rank)


def _padzero_kernel(last_ref, o_ref):
    o_ref[...] = jnp.zeros_like(o_ref)


def _padzero_call(last_blk, nb, tb):
    e = last_blk.shape[0]
    grid_spec = pltpu.PrefetchScalarGridSpec(
        num_scalar_prefetch=1,
        grid=(e,),
        in_specs=[],
        out_specs=pl.BlockSpec((tb * SUBLANES, LANES), lambda i, last: (last[i], 0)),
    )
    return pl.pallas_call(
        _padzero_kernel,
        grid_spec=grid_spec,
        out_shape=jax.ShapeDtypeStruct((nb * tb * SUBLANES, LANES), U32),
        compiler_params=pltpu.CompilerParams(dimension_semantics=("arbitrary",)),
        name="padzero",
    )(last_blk)


def _tile_rows(r):
    return pl.ds(pl.multiple_of(r * SUBLANES, SUBLANES), SUBLANES)


def _dispatch_kernel(dest_ref, h_ref, xs_in, xs_out, sem):
    del xs_in
    tm = dest_ref.shape[1]

    def group(g, carry):
        base = pl.multiple_of(g * ROW_DMA_GROUP, ROW_DMA_GROUP)
        for j in range(ROW_DMA_GROUP):
            for k in range(TOP_K):
                pltpu.make_async_copy(h_ref.at[_tile_rows(base + j)], xs_out.at[_tile_rows(dest_ref[k, base + j])],
                                      sem).start(priority=(j * TOP_K + k) % 2)
        return carry

    lax.fori_loop(0, tm // ROW_DMA_GROUP, group, 0)
    for k in range(TOP_K):
        pltpu.make_async_copy(h_ref, xs_out.at[pl.ds(0, tm * SUBLANES)], sem).wait()


def _dispatch_call(dest, h2p, xs):
    t = dest.shape[1]
    tm = DISP_TM
    return pl.pallas_call(
        _dispatch_kernel,
        grid=(t // tm,),
        in_specs=[
            pl.BlockSpec((TOP_K, tm), lambda i: (0, i), memory_space=pltpu.SMEM),
            pl.BlockSpec((tm * SUBLANES, LANES), lambda i: (i, 0)),
            pl.BlockSpec(memory_space=pl.ANY),
        ],
        out_specs=pl.BlockSpec(memory_space=pl.ANY),
        out_shape=jax.ShapeDtypeStruct(xs.shape, xs.dtype),
        scratch_shapes=[pltpu.SemaphoreType.DMA],
        input_output_aliases={2: 0},
        compiler_params=pltpu.CompilerParams(
            dimension_semantics=("arbitrary",), vmem_limit_bytes=VMEM_LIMIT_BYTES),
        name="dispatch",
    )(dest, h2p, xs)


def _gmm_kernel(be_ref, nxt_ref, nb_ref, x_ref, w1_hbm, w3_hbm, w2_hbm, o_ref,
                st1, st3, st2, sem, wb1, wb3, wb2):
    i = pl.program_id(0)
    half = st1.shape[0] // 2
    tb = x_ref.shape[0] // SUBLANES
    nblk = nb_ref[0]

    def weight_copies(ex):
        return (pltpu.make_async_copy(w1_hbm.at[ex], st1, sem.at[0]),
                pltpu.make_async_copy(w3_hbm.at[ex], st3, sem.at[1]),
                pltpu.make_async_copy(w2_hbm.at[ex], st2, sem.at[2]))

    @pl.when(i == 0)
    def _():
        for cp in weight_copies(be_ref[0]):
            cp.start()

    active = i < nblk
    fresh = jnp.logical_or(i == 0, be_ref[i] != be_ref[jnp.maximum(i - 1, 0)])

    @pl.when(jnp.logical_and(active, fresh))
    def _():
        for cp in weight_copies(be_ref[i]):
            cp.wait()
        n_chunks = st1.shape[0] // CAST_ROWS
        rows2 = st2.shape[0] // n_chunks

        def cast_chunk(c, carry):
            r = pl.multiple_of(c * CAST_ROWS, CAST_ROWS)
            wb1[pl.ds(r, CAST_ROWS), :] = st1[pl.ds(r, CAST_ROWS), :].astype(BF16)
            wb3[pl.ds(r, CAST_ROWS), :] = st3[pl.ds(r, CAST_ROWS), :].astype(BF16)
            r2 = pl.multiple_of(c * rows2, rows2)
            wb2[pl.ds(r2, rows2), :] = st2[pl.ds(r2, rows2), :].astype(BF16)
            return carry

        lax.fori_loop(0, n_chunks, cast_chunk, 0)
        nxt = nxt_ref[i]

        @pl.when(nxt >= 0)
        def _():
            for cp in weight_copies(nxt):
                cp.start()

    @pl.when(active)
    def _():
        xa, xb = _unpack_bf16_pair(_tiles_to_rows(x_ref, tb))
        xa = xa.astype(BF16)
        xb = xb.astype(BF16)
        h1 = _dot(xa, wb1[:half, :]) + _dot(xb, wb1[half:, :])
        h3 = _dot(xa, wb3[:half, :]) + _dot(xb, wb3[half:, :])
        y = _dot((_silu(h1) * h3).astype(BF16), wb2[...])
        _rows_to_tiles(o_ref, _pack_bf16_pair(y[:, :half], y[:, half:]))

    @pl.when(jnp.logical_not(active))
    def _():
        o_ref[...] = jnp.zeros_like(o_ref)


def _gmm_call(block_e, next_e, nblk, xs, w1, w3, w2):
    e, d, de = w1.shape
    assert d // 2 == SUBLANES * LANES
    tb = GMM_TB
    nb = block_e.shape[0]
    blk = (tb * SUBLANES, LANES)
    grid_spec = pltpu.PrefetchScalarGridSpec(
        num_scalar_prefetch=3,
        grid=(nb,),
        in_specs=[
            pl.BlockSpec(blk, lambda i, be, nx, n: (jnp.minimum(i, n[0] - 1), 0)),
            pl.BlockSpec(memory_space=pl.ANY),
            pl.BlockSpec(memory_space=pl.ANY),
            pl.BlockSpec(memory_space=pl.ANY),
        ],
        out_specs=pl.BlockSpec(blk, lambda i, be, nx, n: (jnp.minimum(i, n[0]), 0)),
        scratch_shapes=[
            pltpu.VMEM((d, de), F32),
            pltpu.VMEM((d, de), F32),
            pltpu.VMEM((de, d), F32),
            pltpu.SemaphoreType.DMA((3,)),
            pltpu.VMEM((d, de), BF16),
            pltpu.VMEM((d, de), BF16),
            pltpu.VMEM((de, d), BF16),
        ],
    )
    return pl.pallas_call(
        _gmm_kernel,
        grid_spec=grid_spec,
        out_shape=jax.ShapeDtypeStruct((nb * tb * SUBLANES, LANES), U32),
        compiler_params=pltpu.CompilerParams(
            dimension_semantics=("arbitrary",), vmem_limit_bytes=VMEM_LIMIT_BYTES),
        name="gmm",
    )(block_e, next_e, nblk, xs, w1, w3, w2)


def _final_kernel(destc_ref, destn_ref, gt_ref, x1_ref, h2p_ref, ada_ref, ws1_ref, ws3_ref, ws2_ref,
                  g_ref, b_ref, ys_hbm, o_ref, gbuf, gb_ref, sem):
    i = pl.program_id(0)
    last = pl.num_programs(0) - 1
    tm, d = x1_ref.shape
    half = d // 2
    n_lane_tiles = d // LANES
    slot = lax.rem(i, 2)
    other = 1 - slot

    def start_rows(dest_ref, s, t, j):
        for k in range(TOP_K):
            pltpu.make_async_copy(ys_hbm.at[_tile_rows(dest_ref[k, t])], gbuf.at[s, k, _tile_rows(t)],
                                  sem.at[s, k]).start(priority=(j * TOP_K + k) % 2)

    def wait_rows(s):
        for k in range(TOP_K):
            pltpu.make_async_copy(ys_hbm.at[pl.ds(0, tm * SUBLANES)], gbuf.at[s, k], sem.at[s, k]).wait()

    @pl.when(i == 0)
    def _():
        def group(g, carry):
            base = pl.multiple_of(g * ROW_DMA_GROUP, ROW_DMA_GROUP)
            for j in range(ROW_DMA_GROUP):
                start_rows(destc_ref, 0, base + j, j)
            return carry
        lax.fori_loop(0, tm // ROW_DMA_GROUP, group, 0)

    wait_rows(slot)

    xa, xb = _unpack_bf16_pair(_tiles_to_rows(h2p_ref, tm))
    xa = xa.astype(BF16)
    xb = xb.astype(BF16)
    h1 = _dot(xa, ws1_ref[:half, :]) + _dot(xb, ws1_ref[half:, :])
    h3 = _dot(xa, ws3_ref[:half, :]) + _dot(xb, ws3_ref[half:, :])
    shared = _dot((_silu(h1) * h3).astype(BF16), ws2_ref[...])

    for k in range(TOP_K):
        gb_ref[k] = jnp.broadcast_to(gt_ref[:, k:k + 1], (tm, LANES))
    gate2 = ada_ref[0, 5:6, :]

    per_chunk = tm // SUBLANES
    total = jnp.zeros((tm, LANES), F32)
    for c in range(SUBLANES):
        ra = jnp.zeros((tm, LANES), F32)
        rb = jnp.zeros((tm, LANES), F32)
        for k in range(TOP_K):
            ya, yb = _unpack_bf16_pair(gbuf[slot, k, pl.ds(c, tm, stride=SUBLANES), :])
            ra = ra + ya * gb_ref[k]
            rb = rb + yb * gb_ref[k]
        for r, cs in ((ra, slice(c * LANES, (c + 1) * LANES)), (rb, slice(half + c * LANES, half + (c + 1) * LANES))):
            y = DEEPNORM_ALPHA * x1_ref[:, cs] + gate2[:, cs] * (shared[:, cs] + r)
            o_ref[:, cs] = y
            total = total + y
        for t in range(c * per_chunk, (c + 1) * per_chunk):
            start_rows(destn_ref, other, t, t)
    mu = jnp.sum(total, axis=-1, keepdims=True) / d
    sq = jnp.zeros((tm, LANES), F32)
    for c in range(n_lane_tiles):
        yc = o_ref[:, c * LANES:(c + 1) * LANES] - mu
        sq = sq + yc * yc
    inv = lax.rsqrt(jnp.sum(sq, axis=-1, keepdims=True) / d + LN_EPS)
    for c in range(n_lane_tiles):
        cs = slice(c * LANES, (c + 1) * LANES)
        o_ref[:, cs] = (o_ref[:, cs] - mu) * inv * g_ref[:, cs] + b_ref[:, cs]

    @pl.when(i == last)
    def _():
        wait_rows(other)


def _final_call(dest, gates_t, x1, h2p, ada3, ws1_b, ws3_b, ws2_b, ln2_g, ln2_b, ys, seq):
    t, d = x1.shape
    ds_ = ws1_b.shape[1]
    tm = FIN_TM
    n = t // tm
    per_seq = seq // tm
    const2 = lambda i: (0, 0)
    return pl.pallas_call(
        _final_kernel,
        grid=(n,),
        in_specs=[
            pl.BlockSpec((TOP_K, tm), lambda i: (0, i), memory_space=pltpu.SMEM),
            pl.BlockSpec((TOP_K, tm), lambda i: (0, jnp.minimum(i + 1, n - 1)), memory_space=pltpu.SMEM),
            pl.BlockSpec((tm, TOP_K), lambda i: (i, 0)),
            pl.BlockSpec((tm, d), lambda i: (i, 0)),
            pl.BlockSpec((tm * SUBLANES, LANES), lambda i: (i, 0)),
            pl.BlockSpec((1, 6, d), lambda i: (i // per_seq, 0, 0)),
            pl.BlockSpec((d, ds_), const2),
            pl.BlockSpec((d, ds_), const2),
            pl.BlockSpec((ds_, d), const2),
            pl.BlockSpec((1, d), const2),
            pl.BlockSpec((1, d), const2),
            pl.BlockSpec(memory_space=pl.ANY),
        ],
        out_specs=pl.BlockSpec((tm, d), lambda i: (i, 0)),
        out_shape=jax.ShapeDtypeStruct((t, d), F32),
        scratch_shapes=[
            pltpu.VMEM((2, TOP_K, tm * SUBLANES, LANES), U32),
            pltpu.VMEM((TOP_K, tm, LANES), F32),
            pltpu.SemaphoreType.DMA((2, TOP_K)),
        ],
        compiler_params=pltpu.CompilerParams(
            dimension_semantics=("arbitrary",), vmem_limit_bytes=VMEM_LIMIT_BYTES),
        name="final",
    )(dest, dest, gates_t, x1, h2p, ada3, ws1_b, ws3_b, ws2_b, ln2_g.reshape(1, d), ln2_b.reshape(1, d), ys)


def kernel(x, c, w_ada, b_ada, w_in, b_in, ln_v_g, ln_v_b, w_spatial, b_spatial, w_pool, b_pool, pool_scale, w_out, ln1_g, ln1_b, w_router, router_bias, w1, w3, w2, ws1, ws3, ws2, ln2_g, ln2_b):
    bsz, seq, d = x.shape
    t = bsz * seq
    e = w_router.shape[-1]
    assert w_ada.shape[0] == DEPTH
    l = 0

    ada3 = _ada_call(c, w_ada[l], b_ada[l]).reshape(bsz, 6, d)

    cat = _mix_call(x, ada3, w_in[l].astype(BF16), b_in[l], ln_v_g[l], ln_v_b[l], w_spatial[l],
                    b_spatial[l].T, w_pool[l].astype(BF16), b_pool[l], pool_scale[l])

    wr_t = w_router[l].T
    wr_hi = wr_t.astype(BF16)
    wr_lo = (wr_t - wr_hi.astype(F32)).astype(BF16)
    x1, h2p, logits_t = _out_call(cat.reshape(t, d), x.reshape(t, d), ada3, w_out[l].astype(BF16),
                                  ln1_g[l], ln1_b[l], wr_hi, wr_lo, seq)

    idx, gates, rank, cnt = _route_call(logits_t, router_bias[l])

    tb = GMM_TB
    nb = (t * TOP_K) // tb + e
    dest, last, be, nxt, nblk = _plan_call(cnt, idx, rank, nb, tb)

    xs = _dispatch_call(dest, h2p, _padzero_call(last[:, 0], nb, tb))

    ys = _gmm_call(be[0], nxt[0], nblk[0, :1], xs, w1.reshape(w1.shape[1:]),
                   w3.reshape(w3.shape[1:]), w2.reshape(w2.shape[1:]))

    out = _final_call(dest, gates.T, x1, h2p, ada3, ws1[l].astype(BF16), ws3[l].astype(BF16),
                      ws2[l].astype(BF16), ln2_g[l], ln2_b[l], ys, seq)
    return out.reshape(bsz, seq, d)
```

```python
import functools

import jax
import jax.numpy as jnp
from jax import lax
from jax.experimental import pallas as pl
from jax.experimental.pallas import tpu as pltpu

F32 = jnp.float32
BF16 = jnp.bfloat16
U32 = jnp.uint32
I32 = jnp.int32

N_HEADS_A = 8
GMLP_BLOCK = 128
CHUNK = 64
POOL_WINDOWS = (2, 4, 8, 16)
POOL_HISTORY = 16
N_EXPERT_GROUPS = 8
TOPK_GROUPS = 4
TOP_K = 8
ROUTED_SCALE = 2.5
LN_EPS = 1e-5
DEPTH = 1
DEEPNORM_ALPHA = (2.0 * DEPTH) ** 0.25

LANES = 128
SUBLANES = 8
VMEM_LIMIT_BYTES = 56 * 1024 * 1024
ADA_TN = 1024
MIX_TS = 256
OUT_TM = 256
ROUTE_TR = 256
PLAN_TR = 512
GMM_TB = 256
DISP_TM = 256
FIN_TM = 256
ROW_DMA_GROUP = 16
CAST_ROWS = 256


def _dot(a, b):
    return jnp.dot(a, b, preferred_element_type=F32)


def _split_bf16(a):
    hi = a.astype(BF16)
    lo = (a - hi.astype(F32)).astype(BF16)
    return hi, lo


def _gelu(x):
    return 0.5 * x * (1.0 + lax.erf(x * 0.7071067811865476))


def _silu(x):
    return x * jax.nn.sigmoid(x)


def _pack_bf16_pair(a, b):
    au = pltpu.bitcast(a.astype(BF16).astype(F32), U32)
    bu = pltpu.bitcast(b.astype(BF16).astype(F32), U32)
    return (au & jnp.uint32(0xFFFF0000)) | (bu >> 16)


def _unpack_bf16_pair(w):
    a = pltpu.bitcast(w & jnp.uint32(0xFFFF0000), F32)
    b = pltpu.bitcast(w << 16, F32)
    return a, b


def _tiles_to_rows(ref, n):
    return jnp.concatenate([ref[pl.ds(c, n, stride=SUBLANES), :] for c in range(SUBLANES)], axis=1)


def _rows_to_tiles(ref, value):
    n = value.shape[0]
    for c in range(SUBLANES):
        ref[pl.ds(c, n, stride=SUBLANES), :] = value[:, c * LANES:(c + 1) * LANES]


def _ada_kernel(c_ref, w_ref, b_ref, o_ref):
    s = _silu(c_ref[...])
    sh, sl = _split_bf16(s)
    wh, wl = _split_bf16(w_ref[...])
    o_ref[...] = _dot(sh, wh) + _dot(sl, wh) + _dot(sh, wl) + b_ref[...]


def _ada_call(c, w_ada, b_ada):
    bsz, d = c.shape
    n = w_ada.shape[1]
    return pl.pallas_call(
        _ada_kernel,
        grid=(n // ADA_TN,),
        in_specs=[
            pl.BlockSpec((bsz, d), lambda j: (0, 0)),
            pl.BlockSpec((d, ADA_TN), lambda j: (0, j)),
            pl.BlockSpec((1, ADA_TN), lambda j: (0, j)),
        ],
        out_specs=pl.BlockSpec((bsz, ADA_TN), lambda j: (0, j)),
        out_shape=jax.ShapeDtypeStruct((bsz, n), F32),
        compiler_params=pltpu.CompilerParams(
            dimension_semantics=("arbitrary",), vmem_limit_bytes=VMEM_LIMIT_BYTES),
        name="ada",
    )(c, w_ada, b_ada.reshape(1, n))


def _mix_kernel(x_ref, ada_ref, win_ref, bin_ref, lng_ref, lnb_ref, wsp_ref, bsp_ref,
                wpool_ref, bpool_ref, pscale_ref, o_ref, carry_ref, *, d_a, d_g):
    j = pl.program_id(1)
    ts = x_ref.shape[1]
    x = x_ref[0]
    shift = ada_ref[0, 0:1, :]
    scale = ada_ref[0, 1:2, :]
    h = (x * (1.0 + scale) + shift).astype(BF16)
    proj = _dot(h, win_ref[...]) + bin_ref[...]

    u = _gelu(proj[:, :d_a])
    v = _gelu(proj[:, d_a:2 * d_a])
    mu = jnp.mean(v, axis=-1, keepdims=True)
    vc = v - mu
    var = jnp.mean(vc * vc, axis=-1, keepdims=True)
    vn = (vc * lax.rsqrt(var + LN_EPS) * lng_ref[...] + lnb_ref[...]).astype(BF16)
    row = lax.broadcasted_iota(I32, (GMLP_BLOCK, GMLP_BLOCK), 0)
    col = lax.broadcasted_iota(I32, (GMLP_BLOCK, GMLP_BLOCK), 1)
    causal = (col // CHUNK) <= (row // CHUNK)
    hd = d_a // N_HEADS_A
    for head in range(N_HEADS_A):
        w = jnp.where(causal, wsp_ref[head], 0.0).astype(BF16)
        bias = bsp_ref[:, head:head + 1]
        cs = slice(head * hd, (head + 1) * hd)
        for n in range(ts // GMLP_BLOCK):
            rs = slice(n * GMLP_BLOCK, (n + 1) * GMLP_BLOCK)
            mixed = _dot(w, vn[rs, cs]) + bias
            o_ref[0, rs, cs] = (u[rs, cs] * mixed).astype(BF16)

    z = proj[:, 2 * d_a:]

    @pl.when(j == 0)
    def _():
        carry_ref[...] = jnp.zeros_like(carry_ref)

    ext = jnp.concatenate([carry_ref[...], z], axis=0)
    carry_ref[...] = z[ts - POOL_HISTORY:, :]
    tpos = j * ts + lax.broadcasted_iota(I32, (ts, 1), 0)
    for g, win in enumerate(POOL_WINDOWS):
        gs = slice(g * d_g, (g + 1) * d_g)
        s = ext[:, gs]
        sh = 1
        while sh < win:
            s = s + pltpu.roll(s, sh, axis=0)
            sh *= 2
        cnt = jnp.minimum(tpos + 1, win).astype(F32)
        pooled = s[POOL_HISTORY:, :] / cnt - z[:, gs]
        y = _dot(pooled.astype(BF16), wpool_ref[g]) + bpool_ref[g:g + 1, :]
        o_ref[0, :, d_a + g * d_g:d_a + (g + 1) * d_g] = (y * pscale_ref[:, gs]).astype(BF16)


def _mix_call(x, ada3, w_in_b, b_in, ln_v_g, ln_v_b, w_spatial, b_spatial_t, w_pool_b, b_pool, pool_scale):
    bsz, seq, d = x.shape
    n_proj = w_in_b.shape[1]
    d_a = ln_v_g.shape[0]
    d_b = pool_scale.shape[0]
    n_g, d_g, _ = w_pool_b.shape
    ts = MIX_TS
    const2 = lambda b, j: (0, 0)
    const3 = lambda b, j: (0, 0, 0)
    return pl.pallas_call(
        functools.partial(_mix_kernel, d_a=d_a, d_g=d_g),
        grid=(bsz, seq // ts),
        in_specs=[
            pl.BlockSpec((1, ts, d), lambda b, j: (b, j, 0)),
            pl.BlockSpec((1, 6, d), lambda b, j: (b, 0, 0)),
            pl.BlockSpec((d, n_proj), const2),
            pl.BlockSpec((1, n_proj), const2),
            pl.BlockSpec((1, d_a), const2),
            pl.BlockSpec((1, d_a), const2),
            pl.BlockSpec((N_HEADS_A, GMLP_BLOCK, GMLP_BLOCK), const3),
            pl.BlockSpec((GMLP_BLOCK, N_HEADS_A), const2),
            pl.BlockSpec((n_g, d_g, d_g), const3),
            pl.BlockSpec((n_g, d_g), const2),
            pl.BlockSpec((1, d_b), const2),
        ],
        out_specs=pl.BlockSpec((1, ts, d_a + d_b), lambda b, j: (b, j, 0)),
        out_shape=jax.ShapeDtypeStruct((bsz, seq, d_a + d_b), BF16),
        scratch_shapes=[pltpu.VMEM((POOL_HISTORY, d_b), F32)],
        compiler_params=pltpu.CompilerParams(
            dimension_semantics=("arbitrary", "arbitrary"), vmem_limit_bytes=VMEM_LIMIT_BYTES),
        name="mix",
    )(x, ada3, w_in_b, b_in.reshape(1, n_proj), ln_v_g.reshape(1, d_a), ln_v_b.reshape(1, d_a),
      w_spatial, b_spatial_t, w_pool_b, b_pool, pool_scale.reshape(1, d_b))


def _layer_norm_rows(y, g, b):
    mu = jnp.mean(y, axis=-1, keepdims=True)
    yc = y - mu
    var = jnp.mean(yc * yc, axis=-1, keepdims=True)
    return yc * lax.rsqrt(var + LN_EPS) * g + b


def _out_kernel(cat_ref, x_ref, ada_ref, wout_ref, g_ref, b_ref, wrh_ref, wrl_ref,
                x1_ref, h2p_ref, lgt_ref):
    half = x_ref.shape[1] // 2
    mix = _dot(cat_ref[...], wout_ref[...])
    gate1 = ada_ref[0, 2:3, :]
    x1 = _layer_norm_rows(DEEPNORM_ALPHA * x_ref[...] + gate1 * mix, g_ref[...], b_ref[...])
    x1_ref[...] = x1
    h2 = x1 * (1.0 + ada_ref[0, 4:5, :]) + ada_ref[0, 3:4, :]
    _rows_to_tiles(h2p_ref, _pack_bf16_pair(h2[:, :half], h2[:, half:]))
    hh, hl = _split_bf16(h2)
    nt = (((1,), (1,)), ((), ()))
    wrh = wrh_ref[...]
    lgt_ref[...] = (lax.dot_general(wrh, hh, nt, preferred_element_type=F32)
                    + lax.dot_general(wrh, hl, nt, preferred_element_type=F32)
                    + lax.dot_general(wrl_ref[...], hh, nt, preferred_element_type=F32))


def _out_call(cat, x2d, ada3, w_out_b, ln1_g, ln1_b, wr_hi_t, wr_lo_t, seq):
    t, d = x2d.shape
    e = wr_hi_t.shape[0]
    tm = OUT_TM
    per_seq = seq // tm
    const2 = lambda i: (0, 0)
    return pl.pallas_call(
        _out_kernel,
        grid=(t // tm,),
        in_specs=[
            pl.BlockSpec((tm, d), lambda i: (i, 0)),
            pl.BlockSpec((tm, d), lambda i: (i, 0)),
            pl.BlockSpec((1, 6, d), lambda i: (i // per_seq, 0, 0)),
            pl.BlockSpec((d, d), const2),
            pl.BlockSpec((1, d), const2),
            pl.BlockSpec((1, d), const2),
            pl.BlockSpec((e, d), const2),
            pl.BlockSpec((e, d), const2),
        ],
        out_specs=[
            pl.BlockSpec((tm, d), lambda i: (i, 0)),
            pl.BlockSpec((tm * SUBLANES, LANES), lambda i: (i, 0)),
            pl.BlockSpec((e, tm), lambda i: (0, i)),
        ],
        out_shape=[
            jax.ShapeDtypeStruct((t, d), F32),
            jax.ShapeDtypeStruct((t * SUBLANES, LANES), U32),
            jax.ShapeDtypeStruct((e, t), F32),
        ],
        compiler_params=pltpu.CompilerParams(
            dimension_semantics=("arbitrary",), vmem_limit_bytes=VMEM_LIMIT_BYTES),
        name="out",
    )(cat, x2d, ada3, w_out_b, ln1_g.reshape(1, d), ln1_b.reshape(1, d), wr_hi_t, wr_lo_t)


def _route_kernel(lg_ref, bias_ref, idx_ref, gate_ref, rank_ref, cnt_ref, run_ref, gs_ref, keep_ref):
    i = pl.program_id(0)
    e, tr = lg_ref.shape
    per_group = e // N_EXPERT_GROUPS
    neg = -jnp.inf

    @pl.when(i == 0)
    def _():
        run_ref[...] = jnp.zeros_like(run_ref)

    scores = jax.nn.sigmoid(lg_ref[...])
    sel = scores + bias_ref[...]
    rowi = lax.broadcasted_iota(I32, (e, tr), 0)

    for g in range(N_EXPERT_GROUPS):
        rs = slice(g * per_group, (g + 1) * per_group)
        blk = sel[rs]
        ri = lax.broadcasted_iota(I32, (per_group, tr), 0) + g * per_group
        m1 = jnp.max(blk, axis=0, keepdims=True)
        i1 = jnp.min(jnp.where(blk == m1, ri, e), axis=0, keepdims=True)
        m2 = jnp.max(jnp.where(ri == i1, neg, blk), axis=0, keepdims=True)
        gs_ref[g:g + 1, :] = m1 + m2
    gmat = gs_ref[...]
    gi = lax.broadcasted_iota(I32, gmat.shape, 0)
    beaten = jnp.zeros(gmat.shape, I32)
    for g in range(N_EXPERT_GROUPS):
        gj = gs_ref[g:g + 1, :]
        beats = jnp.where(gj > gmat, 1, jnp.where((gj == gmat) & (gi > g), 1, 0))
        beaten = beaten + beats
    keep_ref[...] = jnp.where(beaten < TOPK_GROUPS, 1.0, 0.0)
    ekeep = jnp.concatenate(
        [jnp.broadcast_to(keep_ref[g:g + 1, :], (per_group, tr)) for g in range(N_EXPERT_GROUPS)], axis=0)
    masked = jnp.where(ekeep > 0.5, sel, neg)

    idxs, tops = [], []
    for k in range(TOP_K):
        m = jnp.max(masked, axis=0, keepdims=True)
        ik = jnp.min(jnp.where(masked == m, rowi, e), axis=0, keepdims=True)
        hit = rowi == ik
        tops.append(jnp.sum(jnp.where(hit, scores, 0.0), axis=0, keepdims=True))
        masked = jnp.where(hit, neg, masked)
        idxs.append(ik)
        idx_ref[k:k + 1, :] = ik
    total = tops[0]
    for k in range(1, TOP_K):
        total = total + tops[k]
    for k in range(TOP_K):
        gate_ref[k:k + 1, :] = tops[k] / total * ROUTED_SCALE

    chosen = jnp.zeros((e, tr), F32)
    for ik in idxs:
        chosen = chosen + jnp.where(rowi == ik, 1.0, 0.0)
    before = lax.broadcasted_iota(I32, (tr, tr), 0) < lax.broadcasted_iota(I32, (tr, tr), 1)
    upper = jnp.where(before, 1.0, 0.0).astype(BF16)
    pos = _dot(chosen.astype(BF16), upper) + run_ref[...]
    for k, ik in enumerate(idxs):
        rank_ref[k:k + 1, :] = jnp.sum(jnp.where(rowi == ik, pos, 0.0), axis=0, keepdims=True).astype(I32)
    run = run_ref[...] + jnp.sum(chosen, axis=1, keepdims=True)
    run_ref[...] = run
    cnt_ref[...] = jnp.broadcast_to(run, cnt_ref.shape).astype(I32)


def _route_call(logits_t, router_bias):
    e, t = logits_t.shape
    tr = ROUTE_TR
    tok = lambda i: (0, i)
    return pl.pallas_call(
        _route_kernel,
        grid=(t // tr,),
        in_specs=[pl.BlockSpec((e, tr), tok), pl.BlockSpec((e, 1), lambda i: (0, 0))],
        out_specs=[
            pl.BlockSpec((TOP_K, tr), tok),
            pl.BlockSpec((TOP_K, tr), tok),
            pl.BlockSpec((TOP_K, tr), tok),
            pl.BlockSpec((e, LANES), lambda i: (0, 0)),
        ],
        out_shape=[
            jax.ShapeDtypeStruct((TOP_K, t), I32),
            jax.ShapeDtypeStruct((TOP_K, t), F32),
            jax.ShapeDtypeStruct((TOP_K, t), I32),
            jax.ShapeDtypeStruct((e, LANES), I32),
        ],
        scratch_shapes=[pltpu.VMEM((e, 1), F32), pltpu.VMEM((N_EXPERT_GROUPS, tr), F32),
                        pltpu.VMEM((N_EXPERT_GROUPS, tr), F32)],
        compiler_params=pltpu.CompilerParams(
            dimension_semantics=("arbitrary",), vmem_limit_bytes=VMEM_LIMIT_BYTES),
        name="route",
    )(logits_t, router_bias.reshape(e, 1))


def _plan_kernel(cnt_ref, idx_ref, rank_ref, dest_ref, last_ref, be_ref, nxt_ref, par_ref, nblk_ref, pst_ref, *, tb):
    e = cnt_ref.shape[0]
    nb = be_ref.shape[1]
    tr = idx_ref.shape[1]

    @pl.when(pl.program_id(0) == 0)
    def _():
        shift = tb.bit_length() - 1
        nblocks = jnp.right_shift(cnt_ref[...] + (tb - 1), shift).astype(F32)
        lower = jnp.where(lax.broadcasted_iota(I32, (e, e), 1) <= lax.broadcasted_iota(I32, (e, e), 0), 1.0, 0.0)
        bend = _dot(lower.astype(BF16), nblocks.astype(BF16))
        nblk = bend[e - 1:e, :]
        pst_ref[...] = ((bend - nblocks) * tb)[:, 0:1]
        last_ref[...] = jnp.where(nblocks > 0, bend - 1.0, nblk).astype(I32)
        nblk_ref[...] = jnp.broadcast_to(nblk, nblk_ref.shape).astype(I32)
        blk = lax.broadcasted_iota(I32, (e, nb), 1).astype(F32)
        exp = lax.broadcasted_iota(I32, (e, nb), 0).astype(F32)
        be = jnp.minimum(jnp.sum(jnp.where(bend[:, 0:1] <= blk, 1.0, 0.0), axis=0, keepdims=True), e - 1.0)
        later = jnp.where((nblocks[:, 0:1] > 0) & (exp > be), exp, float(e))
        nxt = jnp.min(later, axis=0, keepdims=True)
        nxt = jnp.where(nxt >= e, -1.0, nxt)
        earlier = jnp.sum(jnp.where((nblocks[:, 0:1] > 0) & (exp < be), 1.0, 0.0), axis=0, keepdims=True)
        be_ref[...] = jnp.broadcast_to(be, be_ref.shape).astype(I32)
        nxt_ref[...] = jnp.broadcast_to(nxt, nxt_ref.shape).astype(I32)
        par_ref[...] = jnp.bitwise_and(jnp.broadcast_to(earlier, par_ref.shape).astype(I32), 1)

    rowi = lax.broadcasted_iota(I32, (e, tr), 0)
    pst = pst_ref[...]
    for k in range(TOP_K):
        start = jnp.sum(jnp.where(rowi == idx_ref[k:k + 1, :], pst, 0.0), axis=0, keepdims=True)
        dest_ref[k:k + 1, :] = start.astype(I32) + rank_ref[k:k + 1, :]


def _plan_call(cnt, idx, rank, nb, tb):
    e = cnt.shape[0]
    t = idx.shape[1]
    tr = PLAN_TR
    tok = lambda i: (0, i)
    const = lambda i: (0, 0)
    return pl.pallas_call(
        functools.partial(_plan_kernel, tb=tb),
        grid=(t // tr,),
        in_specs=[pl.BlockSpec((e, LANES), const), pl.BlockSpec((TOP_K, tr), tok), pl.BlockSpec((TOP_K, tr), tok)],
        out_specs=[
            pl.BlockSpec((TOP_K, tr), tok),
            pl.BlockSpec((e, LANES), const),
            pl.BlockSpec((SUBLANES, nb), const),
            pl.BlockSpec((SUBLANES, nb), const),
            pl.BlockSpec((SUBLANES, nb), const),
            pl.BlockSpec((SUBLANES, LANES), const),
        ],
        out_shape=[
            jax.ShapeDtypeStruct((TOP_K, t), I32),
            jax.ShapeDtypeStruct((e, LANES), I32),
            jax.ShapeDtypeStruct((SUBLANES, nb), I32),
            jax.ShapeDtypeStruct((SUBLANES, nb), I32),
            jax.ShapeDtypeStruct((SUBLANES, nb), I32),
            jax.ShapeDtypeStruct((SUBLANES, LANES), I32),
        ],
        scratch_shapes=[pltpu.VMEM((e, 1), F32)],
        compiler_params=pltpu.CompilerParams(dimension_semantics=("arbitrary",)),
        name="plan",
    )(cnt, idx, rank)


def _padzero_kernel(last_ref, o_ref):
    o_ref[...] = jnp.zeros_like(o_ref)


def _padzero_call(last_blk, nb, tb):
    e = last_blk.shape[0]
    grid_spec = pltpu.PrefetchScalarGridSpec(
        num_scalar_prefetch=1,
        grid=(e,),
        in_specs=[],
        out_specs=pl.BlockSpec((tb * SUBLANES, LANES), lambda i, last: (last[i], 0)),
    )
    return pl.pallas_call(
        _padzero_kernel,
        grid_spec=grid_spec,
        out_shape=jax.ShapeDtypeStruct((nb * tb * SUBLANES, LANES), U32),
        compiler_params=pltpu.CompilerParams(dimension_semantics=("arbitrary",)),
        name="padzero",
    )(last_blk)


def _tile_rows(r):
    return pl.ds(pl.multiple_of(r * SUBLANES, SUBLANES), SUBLANES)


def _dispatch_kernel(dest_ref, h_ref, xs_in, xs_out, sem):
    del xs_in
    tm = dest_ref.shape[1]

    def group(g, carry):
        base = pl.multiple_of(g * ROW_DMA_GROUP, ROW_DMA_GROUP)
        for j in range(ROW_DMA_GROUP):
            for k in range(TOP_K):
                pltpu.make_async_copy(h_ref.at[_tile_rows(base + j)], xs_out.at[_tile_rows(dest_ref[k, base + j])],
                                      sem).start(priority=(j * TOP_K + k) % 2)
        return carry

    lax.fori_loop(0, tm // ROW_DMA_GROUP, group, 0)
    for k in range(TOP_K):
        pltpu.make_async_copy(h_ref, xs_out.at[pl.ds(0, tm * SUBLANES)], sem).wait()


def _dispatch_call(dest, h2p, xs):
    t = dest.shape[1]
    tm = DISP_TM
    return pl.pallas_call(
        _dispatch_kernel,
        grid=(t // tm,),
        in_specs=[
            pl.BlockSpec((TOP_K, tm), lambda i: (0, i), memory_space=pltpu.SMEM),
            pl.BlockSpec((tm * SUBLANES, LANES), lambda i: (i, 0)),
            pl.BlockSpec(memory_space=pl.ANY),
        ],
        out_specs=pl.BlockSpec(memory_space=pl.ANY),
        out_shape=jax.ShapeDtypeStruct(xs.shape, xs.dtype),
        scratch_shapes=[pltpu.SemaphoreType.DMA],
        input_output_aliases={2: 0},
        compiler_params=pltpu.CompilerParams(
            dimension_semantics=("arbitrary",), vmem_limit_bytes=VMEM_LIMIT_BYTES),
        name="dispatch",
    )(dest, h2p, xs)


def _gmm_kernel(be_ref, nxt_ref, par_ref, nb_ref, x_ref, w1_hbm, w3_hbm, w2_hbm, o_ref,
                st1, st3, st2, sem, wb1, wb3, wb2):
    i = pl.program_id(0)
    half = st1.shape[1] // 2
    tb = x_ref.shape[0] // SUBLANES
    nblk = nb_ref[0]

    def weight_copies(ex, s):
        return (pltpu.make_async_copy(w1_hbm.at[ex], st1.at[s], sem.at[s, 0]),
                pltpu.make_async_copy(w3_hbm.at[ex], st3.at[s], sem.at[s, 1]),
                pltpu.make_async_copy(w2_hbm.at[ex], st2.at[s], sem.at[s, 2]))

    @pl.when(i == 0)
    def _():
        for cp in weight_copies(be_ref[0], par_ref[0]):
            cp.start()

    active = i < nblk
    fresh = jnp.logical_or(i == 0, be_ref[i] != be_ref[jnp.maximum(i - 1, 0)])

    @pl.when(jnp.logical_and(active, fresh))
    def _():
        s = par_ref[i]
        nxt = nxt_ref[i]

        @pl.when(nxt >= 0)
        def _():
            for cp in weight_copies(nxt, 1 - s):
                cp.start()

        for cp in weight_copies(be_ref[i], s):
            cp.wait()

        xa, xb = _unpack_bf16_pair(_tiles_to_rows(x_ref, tb))
        xs = (xa.astype(BF16), xb.astype(BF16))
        n_chunks = st1.shape[1] // CAST_ROWS
        per_half = n_chunks // 2
        rows2 = st2.shape[1] // n_chunks
        h1 = None
        h3 = None
        for c in range(n_chunks):
            rs = slice(c * CAST_ROWS, (c + 1) * CAST_ROWS)
            wb1[rs, :] = st1[s, rs, :].astype(BF16)
            wb3[rs, :] = st3[s, rs, :].astype(BF16)
            rs2 = slice(c * rows2, (c + 1) * rows2)
            wb2[rs2, :] = st2[s, rs2, :].astype(BF16)
            xc = xs[c // per_half][:, (c % per_half) * CAST_ROWS:(c % per_half + 1) * CAST_ROWS]
            p1 = _dot(xc, wb1[rs, :])
            p3 = _dot(xc, wb3[rs, :])
            h1 = p1 if h1 is None else h1 + p1
            h3 = p3 if h3 is None else h3 + p3
        y = _dot((_silu(h1) * h3).astype(BF16), wb2[...])
        _rows_to_tiles(o_ref, _pack_bf16_pair(y[:, :half], y[:, half:]))

    @pl.when(jnp.logical_and(active, jnp.logical_not(fresh)))
    def _():
        xa, xb = _unpack_bf16_pair(_tiles_to_rows(x_ref, tb))
        xa = xa.astype(BF16)
        xb = xb.astype(BF16)
        h1 = _dot(xa, wb1[:half, :]) + _dot(xb, wb1[half:, :])
        h3 = _dot(xa, wb3[:half, :]) + _dot(xb, wb3[half:, :])
        y = _dot((_silu(h1) * h3).astype(BF16), wb2[...])
        _rows_to_tiles(o_ref, _pack_bf16_pair(y[:, :half], y[:, half:]))

    @pl.when(jnp.logical_not(active))
    def _():
        o_ref[...] = jnp.zeros_like(o_ref)


def _gmm_call(block_e, next_e, parity, nblk, xs, w1, w3, w2):
    e, d, de = w1.shape
    assert d // 2 == SUBLANES * LANES
    tb = GMM_TB
    nb = block_e.shape[0]
    blk = (tb * SUBLANES, LANES)
    grid_spec = pltpu.PrefetchScalarGridSpec(
        num_scalar_prefetch=4,
        grid=(nb,),
        in_specs=[
            pl.BlockSpec(blk, lambda i, be, nx, pa, n: (jnp.minimum(i, n[0] - 1), 0)),
            pl.BlockSpec(memory_space=pl.ANY),
            pl.BlockSpec(memory_space=pl.ANY),
            pl.BlockSpec(memory_space=pl.ANY),
        ],
        out_specs=pl.BlockSpec(blk, lambda i, be, nx, pa, n: (jnp.minimum(i, n[0]), 0)),
        scratch_shapes=[
            pltpu.VMEM((2, d, de), F32),
            pltpu.VMEM((2, d, de), F32),
            pltpu.VMEM((2, de, d), F32),
            pltpu.SemaphoreType.DMA((2, 3)),
            pltpu.VMEM((d, de), BF16),
            pltpu.VMEM((d, de), BF16),
            pltpu.VMEM((de, d), BF16),
        ],
    )
    return pl.pallas_call(
        _gmm_kernel,
        grid_spec=grid_spec,
        out_shape=jax.ShapeDtypeStruct((nb * tb * SUBLANES, LANES), U32),
        compiler_params=pltpu.CompilerParams(
            dimension_semantics=("arbitrary",), vmem_limit_bytes=VMEM_LIMIT_BYTES),
        name="gmm",
    )(block_e, next_e, parity, nblk, xs, w1, w3, w2)


def _final_kernel(destc_ref, destn_ref, gt_ref, x1_ref, h2p_ref, ada_ref, ws1_ref, ws3_ref, ws2_ref,
                  g_ref, b_ref, ys_hbm, o_ref, gbuf, gb_ref, sem):
    i = pl.program_id(0)
    last = pl.num_programs(0) - 1
    tm, d = x1_ref.shape
    half = d // 2
    n_lane_tiles = d // LANES
    slot = lax.rem(i, 2)
    other = 1 - slot

    def start_rows(dest_ref, s, t, j):
        for k in range(TOP_K):
            pltpu.make_async_copy(ys_hbm.at[_tile_rows(dest_ref[k, t])], gbuf.at[s, k, _tile_rows(t)],
                                  sem.at[s, k]).start(priority=(j * TOP_K + k) % 2)

    def wait_rows(s):
        for k in range(TOP_K):
            pltpu.make_async_copy(ys_hbm.at[pl.ds(0, tm * SUBLANES)], gbuf.at[s, k], sem.at[s, k]).wait()

    @pl.when(i == 0)
    def _():
        def group(g, carry):
            base = pl.multiple_of(g * ROW_DMA_GROUP, ROW_DMA_GROUP)
            for j in range(ROW_DMA_GROUP):
                start_rows(destc_ref, 0, base + j, j)
            return carry
        lax.fori_loop(0, tm // ROW_DMA_GROUP, group, 0)

    wait_rows(slot)

    xa, xb = _unpack_bf16_pair(_tiles_to_rows(h2p_ref, tm))
    xa = xa.astype(BF16)
    xb = xb.astype(BF16)
    h1 = _dot(xa, ws1_ref[:half, :]) + _dot(xb, ws1_ref[half:, :])
    h3 = _dot(xa, ws3_ref[:half, :]) + _dot(xb, ws3_ref[half:, :])
    shared = _dot((_silu(h1) * h3).astype(BF16), ws2_ref[...])

    for k in range(TOP_K):
        gb_ref[k] = jnp.broadcast_to(gt_ref[:, k:k + 1], (tm, LANES))
    gate2 = ada_ref[0, 5:6, :]

    per_chunk = tm // SUBLANES
    total = jnp.zeros((tm, LANES), F32)
    for c in range(SUBLANES):
        ra = jnp.zeros((tm, LANES), F32)
        rb = jnp.zeros((tm, LANES), F32)
        for k in range(TOP_K):
            ya, yb = _unpack_bf16_pair(gbuf[slot, k, pl.ds(c, tm, stride=SUBLANES), :])
            ra = ra + ya * gb_ref[k]
            rb = rb + yb * gb_ref[k]
        for r, cs in ((ra, slice(c * LANES, (c + 1) * LANES)), (rb, slice(half + c * LANES, half + (c + 1) * LANES))):
            y = DEEPNORM_ALPHA * x1_ref[:, cs] + gate2[:, cs] * (shared[:, cs] + r)
            o_ref[:, cs] = y
            total = total + y
        for t in range(c * per_chunk, (c + 1) * per_chunk):
            start_rows(destn_ref, other, t, t)
    mu = jnp.sum(total, axis=-1, keepdims=True) / d
    sq = jnp.zeros((tm, LANES), F32)
    for c in range(n_lane_tiles):
        yc = o_ref[:, c * LANES:(c + 1) * LANES] - mu
        sq = sq + yc * yc
    inv = lax.rsqrt(jnp.sum(sq, axis=-1, keepdims=True) / d + LN_EPS)
    for c in range(n_lane_tiles):
        cs = slice(c * LANES, (c + 1) * LANES)
        o_ref[:, cs] = (o_ref[:, cs] - mu) * inv * g_ref[:, cs] + b_ref[:, cs]

    @pl.when(i == last)
    def _():
        wait_rows(other)


def _final_call(dest, gates_t, x1, h2p, ada3, ws1_b, ws3_b, ws2_b, ln2_g, ln2_b, ys, seq):
    t, d = x1.shape
    ds_ = ws1_b.shape[1]
    tm = FIN_TM
    n = t // tm
    per_seq = seq // tm
    const2 = lambda i: (0, 0)
    return pl.pallas_call(
        _final_kernel,
        grid=(n,),
        in_specs=[
            pl.BlockSpec((TOP_K, tm), lambda i: (0, i), memory_space=pltpu.SMEM),
            pl.BlockSpec((TOP_K, tm), lambda i: (0, jnp.minimum(i + 1, n - 1)), memory_space=pltpu.SMEM),
            pl.BlockSpec((tm, TOP_K), lambda i: (i, 0)),
            pl.BlockSpec((tm, d), lambda i: (i, 0)),
            pl.BlockSpec((tm * SUBLANES, LANES), lambda i: (i, 0)),
            pl.BlockSpec((1, 6, d), lambda i: (i // per_seq, 0, 0)),
            pl.BlockSpec((d, ds_), const2),
            pl.BlockSpec((d, ds_), const2),
            pl.BlockSpec((ds_, d), const2),
            pl.BlockSpec((1, d), const2),
            pl.BlockSpec((1, d), const2),
            pl.BlockSpec(memory_space=pl.ANY),
        ],
        out_specs=pl.BlockSpec((tm, d), lambda i: (i, 0)),
        out_shape=jax.ShapeDtypeStruct((t, d), F32),
        scratch_shapes=[
            pltpu.VMEM((2, TOP_K, tm * SUBLANES, LANES), U32),
            pltpu.VMEM((TOP_K, tm, LANES), F32),
            pltpu.SemaphoreType.DMA((2, TOP_K)),
        ],
        compiler_params=pltpu.CompilerParams(
            dimension_semantics=("arbitrary",), vmem_limit_bytes=VMEM_LIMIT_BYTES),
        name="final",
    )(dest, dest, gates_t, x1, h2p, ada3, ws1_b, ws3_b, ws2_b, ln2_g.reshape(1, d), ln2_b.reshape(1, d), ys)


def kernel(x, c, w_ada, b_ada, w_in, b_in, ln_v_g, ln_v_b, w_spatial, b_spatial, w_pool, b_pool, pool_scale, w_out, ln1_g, ln1_b, w_router, router_bias, w1, w3, w2, ws1, ws3, ws2, ln2_g, ln2_b):
    bsz, seq, d = x.shape
    t = bsz * seq
    e = w_router.shape[-1]
    assert w_ada.shape[0] == DEPTH
    l = 0

    ada3 = _ada_call(c, w_ada[l], b_ada[l]).reshape(bsz, 6, d)

    cat = _mix_call(x, ada3, w_in[l].astype(BF16), b_in[l], ln_v_g[l], ln_v_b[l], w_spatial[l],
                    b_spatial[l].T, w_pool[l].astype(BF16), b_pool[l], pool_scale[l])

    wr_t = w_router[l].T
    wr_hi = wr_t.astype(BF16)
    wr_lo = (wr_t - wr_hi.astype(F32)).astype(BF16)
    x1, h2p, logits_t = _out_call(cat.reshape(t, d), x.reshape(t, d), ada3, w_out[l].astype(BF16),
                                  ln1_g[l], ln1_b[l], wr_hi, wr_lo, seq)

    idx, gates, rank, cnt = _route_call(logits_t, router_bias[l])

    tb = GMM_TB
    nb = (t * TOP_K) // tb + e
    dest, last, be, nxt, par, nblk = _plan_call(cnt, idx, rank, nb, tb)

    xs = _dispatch_call(dest, h2p, _padzero_call(last[:, 0], nb, tb))

    ys = _gmm_call(be[0], nxt[0], par[0], nblk[0, :1], xs, w1.reshape(w1.shape[1:]),
                   w3.reshape(w3.shape[1:]), w2.reshape(w2.shape[1:]))

    out = _final_call(dest, gates.T, x1, h2p, ada3, ws1[l].astype(BF16), ws3[l].astype(BF16),
                      ws2[l].astype(BF16), ln2_g[l], ln2_b[l], ys, seq)
    return out.reshape(bsz, seq, d)
```

```python
import functools

import jax
import jax.numpy as jnp
from jax import lax
from jax.experimental import pallas as pl
from jax.experimental.pallas import tpu as pltpu

F32 = jnp.float32
BF16 = jnp.bfloat16
U32 = jnp.uint32
I32 = jnp.int32

N_HEADS_A = 8
GMLP_BLOCK = 128
CHUNK = 64
POOL_WINDOWS = (2, 4, 8, 16)
POOL_HISTORY = 16
N_EXPERT_GROUPS = 8
TOPK_GROUPS = 4
TOP_K = 8
ROUTED_SCALE = 2.5
LN_EPS = 1e-5
DEPTH = 1
DEEPNORM_ALPHA = (2.0 * DEPTH) ** 0.25

LANES = 128
SUBLANES = 8
VMEM_LIMIT_BYTES = 56 * 1024 * 1024
ADA_TN = 1024
MIX_TS = 256
OUT_TM = 256
ROUTE_TR = 256
PLAN_TR = 512
GMM_TB = 256
DISP_TM = 256
FIN_TM = 256
ROW_DMA_GROUP = 16
PADZERO_GROUP = 8
WEIGHT_DMA_PRIORITY = 1
CAST_ROWS = 256


def _dot(a, b):
    return jnp.dot(a, b, preferred_element_type=F32)


def _split_bf16(a):
    hi = a.astype(BF16)
    lo = (a - hi.astype(F32)).astype(BF16)
    return hi, lo


def _gelu(x):
    return 0.5 * x * (1.0 + lax.erf(x * 0.7071067811865476))


def _silu(x):
    return x * jax.nn.sigmoid(x)


def _pack_bf16_pair(a, b):
    au = pltpu.bitcast(a.astype(BF16).astype(F32), U32)
    bu = pltpu.bitcast(b.astype(BF16).astype(F32), U32)
    return (au & jnp.uint32(0xFFFF0000)) | (bu >> 16)


def _unpack_bf16_pair(w):
    a = pltpu.bitcast(w & jnp.uint32(0xFFFF0000), F32)
    b = pltpu.bitcast(w << 16, F32)
    return a, b


def _tiles_to_rows(ref, n):
    return jnp.concatenate([ref[pl.ds(c, n, stride=SUBLANES), :] for c in range(SUBLANES)], axis=1)


def _rows_to_tiles(ref, value):
    n = value.shape[0]
    for c in range(SUBLANES):
        ref[pl.ds(c, n, stride=SUBLANES), :] = value[:, c * LANES:(c + 1) * LANES]


def _ada_kernel(c_ref, w_ref, b_ref, o_ref):
    s = _silu(c_ref[...])
    sh, sl = _split_bf16(s)
    wh, wl = _split_bf16(w_ref[...])
    o_ref[...] = _dot(sh, wh) + _dot(sl, wh) + _dot(sh, wl) + b_ref[...]


def _ada_call(c, w_ada, b_ada):
    bsz, d = c.shape
    n = w_ada.shape[1]
    return pl.pallas_call(
        _ada_kernel,
        grid=(n // ADA_TN,),
        in_specs=[
            pl.BlockSpec((bsz, d), lambda j: (0, 0)),
            pl.BlockSpec((d, ADA_TN), lambda j: (0, j)),
            pl.BlockSpec((1, ADA_TN), lambda j: (0, j)),
        ],
        out_specs=pl.BlockSpec((bsz, ADA_TN), lambda j: (0, j)),
        out_shape=jax.ShapeDtypeStruct((bsz, n), F32),
        compiler_params=pltpu.CompilerParams(
            dimension_semantics=("arbitrary",), vmem_limit_bytes=VMEM_LIMIT_BYTES),
        name="ada",
    )(c, w_ada, b_ada.reshape(1, n))


def _mix_kernel(x_ref, ada_ref, win_ref, bin_ref, lng_ref, lnb_ref, wsp_ref, bsp_ref,
                wpool_ref, bpool_ref, pscale_ref, o_ref, carry_ref, *, d_a, d_g):
    j = pl.program_id(1)
    ts = x_ref.shape[1]
    x = x_ref[0]
    shift = ada_ref[0, 0:1, :]
    scale = ada_ref[0, 1:2, :]
    h = (x * (1.0 + scale) + shift).astype(BF16)
    proj = _dot(h, win_ref[...]) + bin_ref[...]

    u = _gelu(proj[:, :d_a])
    v = _gelu(proj[:, d_a:2 * d_a])
    mu = jnp.mean(v, axis=-1, keepdims=True)
    vc = v - mu
    var = jnp.mean(vc * vc, axis=-1, keepdims=True)
    vn = (vc * lax.rsqrt(var + LN_EPS) * lng_ref[...] + lnb_ref[...]).astype(BF16)
    row = lax.broadcasted_iota(I32, (GMLP_BLOCK, GMLP_BLOCK), 0)
    col = lax.broadcasted_iota(I32, (GMLP_BLOCK, GMLP_BLOCK), 1)
    causal = (col // CHUNK) <= (row // CHUNK)
    hd = d_a // N_HEADS_A
    for head in range(N_HEADS_A):
        w = jnp.where(causal, wsp_ref[head], 0.0).astype(BF16)
        bias = bsp_ref[:, head:head + 1]
        cs = slice(head * hd, (head + 1) * hd)
        for n in range(ts // GMLP_BLOCK):
            rs = slice(n * GMLP_BLOCK, (n + 1) * GMLP_BLOCK)
            mixed = _dot(w, vn[rs, cs]) + bias
            o_ref[0, rs, cs] = (u[rs, cs] * mixed).astype(BF16)

    z = proj[:, 2 * d_a:]

    @pl.when(j == 0)
    def _():
        carry_ref[...] = jnp.zeros_like(carry_ref)

    ext = jnp.concatenate([carry_ref[...], z], axis=0)
    carry_ref[...] = z[ts - POOL_HISTORY:, :]
    tpos = j * ts + lax.broadcasted_iota(I32, (ts, 1), 0)
    for g, win in enumerate(POOL_WINDOWS):
        gs = slice(g * d_g, (g + 1) * d_g)
        s = ext[:, gs]
        sh = 1
        while sh < win:
            s = s + pltpu.roll(s, sh, axis=0)
            sh *= 2
        cnt = jnp.minimum(tpos + 1, win).astype(F32)
        pooled = s[POOL_HISTORY:, :] / cnt - z[:, gs]
        y = _dot(pooled.astype(BF16), wpool_ref[g]) + bpool_ref[g:g + 1, :]
        o_ref[0, :, d_a + g * d_g:d_a + (g + 1) * d_g] = (y * pscale_ref[:, gs]).astype(BF16)


def _mix_call(x, ada3, w_in_b, b_in, ln_v_g, ln_v_b, w_spatial, b_spatial_t, w_pool_b, b_pool, pool_scale):
    bsz, seq, d = x.shape
    n_proj = w_in_b.shape[1]
    d_a = ln_v_g.shape[0]
    d_b = pool_scale.shape[0]
    n_g, d_g, _ = w_pool_b.shape
    ts = MIX_TS
    const2 = lambda b, j: (0, 0)
    const3 = lambda b, j: (0, 0, 0)
    return pl.pallas_call(
        functools.partial(_mix_kernel, d_a=d_a, d_g=d_g),
        grid=(bsz, seq // ts),
        in_specs=[
            pl.BlockSpec((1, ts, d), lambda b, j: (b, j, 0)),
            pl.BlockSpec((1, 6, d), lambda b, j: (b, 0, 0)),
            pl.BlockSpec((d, n_proj), const2),
            pl.BlockSpec((1, n_proj), const2),
            pl.BlockSpec((1, d_a), const2),
            pl.BlockSpec((1, d_a), const2),
            pl.BlockSpec((N_HEADS_A, GMLP_BLOCK, GMLP_BLOCK), const3),
            pl.BlockSpec((GMLP_BLOCK, N_HEADS_A), const2),
            pl.BlockSpec((n_g, d_g, d_g), const3),
            pl.BlockSpec((n_g, d_g), const2),
            pl.BlockSpec((1, d_b), const2),
        ],
        out_specs=pl.BlockSpec((1, ts, d_a + d_b), lambda b, j: (b, j, 0)),
        out_shape=jax.ShapeDtypeStruct((bsz, seq, d_a + d_b), BF16),
        scratch_shapes=[pltpu.VMEM((POOL_HISTORY, d_b), F32)],
        compiler_params=pltpu.CompilerParams(
            dimension_semantics=("arbitrary", "arbitrary"), vmem_limit_bytes=VMEM_LIMIT_BYTES),
        name="mix",
    )(x, ada3, w_in_b, b_in.reshape(1, n_proj), ln_v_g.reshape(1, d_a), ln_v_b.reshape(1, d_a),
      w_spatial, b_spatial_t, w_pool_b, b_pool, pool_scale.reshape(1, d_b))


def _layer_norm_rows(y, g, b):
    mu = jnp.mean(y, axis=-1, keepdims=True)
    yc = y - mu
    var = jnp.mean(yc * yc, axis=-1, keepdims=True)
    return yc * lax.rsqrt(var + LN_EPS) * g + b


def _out_kernel(cat_ref, x_ref, ada_ref, wout_ref, g_ref, b_ref, wrh_ref, wrl_ref,
                x1_ref, h2p_ref, lgt_ref):
    half = x_ref.shape[1] // 2
    mix = _dot(cat_ref[...], wout_ref[...])
    gate1 = ada_ref[0, 2:3, :]
    x1 = _layer_norm_rows(DEEPNORM_ALPHA * x_ref[...] + gate1 * mix, g_ref[...], b_ref[...])
    x1_ref[...] = x1
    h2 = x1 * (1.0 + ada_ref[0, 4:5, :]) + ada_ref[0, 3:4, :]
    _rows_to_tiles(h2p_ref, _pack_bf16_pair(h2[:, :half], h2[:, half:]))
    hh, hl = _split_bf16(h2)
    nt = (((1,), (1,)), ((), ()))
    wrh = wrh_ref[...]
    lgt_ref[...] = (lax.dot_general(wrh, hh, nt, preferred_element_type=F32)
                    + lax.dot_general(wrh, hl, nt, preferred_element_type=F32)
                    + lax.dot_general(wrl_ref[...], hh, nt, preferred_element_type=F32))


def _out_call(cat, x2d, ada3, w_out_b, ln1_g, ln1_b, wr_hi_t, wr_lo_t, seq):
    t, d = x2d.shape
    e = wr_hi_t.shape[0]
    tm = OUT_TM
    per_seq = seq // tm
    const2 = lambda i: (0, 0)
    return pl.pallas_call(
        _out_kernel,
        grid=(t // tm,),
        in_specs=[
            pl.BlockSpec((tm, d), lambda i: (i, 0)),
            pl.BlockSpec((tm, d), lambda i: (i, 0)),
            pl.BlockSpec((1, 6, d), lambda i: (i // per_seq, 0, 0)),
            pl.BlockSpec((d, d), const2),
            pl.BlockSpec((1, d), const2),
            pl.BlockSpec((1, d), const2),
            pl.BlockSpec((e, d), const2),
            pl.BlockSpec((e, d), const2),
        ],
        out_specs=[
            pl.BlockSpec((tm, d), lambda i: (i, 0)),
            pl.BlockSpec((tm * SUBLANES, LANES), lambda i: (i, 0)),
            pl.BlockSpec((e, tm), lambda i: (0, i)),
        ],
        out_shape=[
            jax.ShapeDtypeStruct((t, d), F32),
            jax.ShapeDtypeStruct((t * SUBLANES, LANES), U32),
            jax.ShapeDtypeStruct((e, t), F32),
        ],
        compiler_params=pltpu.CompilerParams(
            dimension_semantics=("arbitrary",), vmem_limit_bytes=VMEM_LIMIT_BYTES),
        name="out",
    )(cat, x2d, ada3, w_out_b, ln1_g.reshape(1, d), ln1_b.reshape(1, d), wr_hi_t, wr_lo_t)


def _route_kernel(lg_ref, bias_ref, idx_ref, gate_ref, rank_ref, cnt_ref, run_ref, gs_ref, keep_ref):
    i = pl.program_id(0)
    e, tr = lg_ref.shape
    per_group = e // N_EXPERT_GROUPS
    neg = -jnp.inf

    @pl.when(i == 0)
    def _():
        run_ref[...] = jnp.zeros_like(run_ref)

    scores = jax.nn.sigmoid(lg_ref[...])
    sel = scores + bias_ref[...]
    rowi = lax.broadcasted_iota(I32, (e, tr), 0)

    for g in range(N_EXPERT_GROUPS):
        rs = slice(g * per_group, (g + 1) * per_group)
        blk = sel[rs]
        ri = lax.broadcasted_iota(I32, (per_group, tr), 0) + g * per_group
        m1 = jnp.max(blk, axis=0, keepdims=True)
        i1 = jnp.min(jnp.where(blk == m1, ri, e), axis=0, keepdims=True)
        m2 = jnp.max(jnp.where(ri == i1, neg, blk), axis=0, keepdims=True)
        gs_ref[g:g + 1, :] = m1 + m2
    gmat = gs_ref[...]
    gi = lax.broadcasted_iota(I32, gmat.shape, 0)
    beaten = jnp.zeros(gmat.shape, I32)
    for g in range(N_EXPERT_GROUPS):
        gj = gs_ref[g:g + 1, :]
        beats = jnp.where(gj > gmat, 1, jnp.where((gj == gmat) & (gi > g), 1, 0))
        beaten = beaten + beats
    keep_ref[...] = jnp.where(beaten < TOPK_GROUPS, 1.0, 0.0)
    ekeep = jnp.concatenate(
        [jnp.broadcast_to(keep_ref[g:g + 1, :], (per_group, tr)) for g in range(N_EXPERT_GROUPS)], axis=0)
    masked = jnp.where(ekeep > 0.5, sel, neg)

    idxs, tops = [], []
    for k in range(TOP_K):
        m = jnp.max(masked, axis=0, keepdims=True)
        ik = jnp.min(jnp.where(masked == m, rowi, e), axis=0, keepdims=True)
        hit = rowi == ik
        tops.append(jnp.sum(jnp.where(hit, scores, 0.0), axis=0, keepdims=True))
        masked = jnp.where(hit, neg, masked)
        idxs.append(ik)
        idx_ref[k:k + 1, :] = ik
    total = tops[0]
    for k in range(1, TOP_K):
        total = total + tops[k]
    for k in range(TOP_K):
        gate_ref[k:k + 1, :] = tops[k] / total * ROUTED_SCALE

    chosen = jnp.zeros((e, tr), F32)
    for ik in idxs:
        chosen = chosen + jnp.where(rowi == ik, 1.0, 0.0)
    before = lax.broadcasted_iota(I32, (tr, tr), 0) < lax.broadcasted_iota(I32, (tr, tr), 1)
    upper = jnp.where(before, 1.0, 0.0).astype(BF16)
    pos = _dot(chosen.astype(BF16), upper) + run_ref[...]
    for k, ik in enumerate(idxs):
        rank_ref[k:k + 1, :] = jnp.sum(jnp.where(rowi == ik, pos, 0.0), axis=0, keepdims=True).astype(I32)
    run = run_ref[...] + jnp.sum(chosen, axis=1, keepdims=True)
    run_ref[...] = run
    cnt_ref[...] = jnp.broadcast_to(run, cnt_ref.shape).astype(I32)


def _route_call(logits_t, router_bias):
    e, t = logits_t.shape
    tr = ROUTE_TR
    tok = lambda i: (0, i)
    return pl.pallas_call(
        _route_kernel,
        grid=(t // tr,),
        in_specs=[pl.BlockSpec((e, tr), tok), pl.BlockSpec((e, 1), lambda i: (0, 0))],
        out_specs=[
            pl.BlockSpec((TOP_K, tr), tok),
            pl.BlockSpec((TOP_K, tr), tok),
            pl.BlockSpec((TOP_K, tr), tok),
            pl.BlockSpec((e, LANES), lambda i: (0, 0)),
        ],
        out_shape=[
            jax.ShapeDtypeStruct((TOP_K, t), I32),
            jax.ShapeDtypeStruct((TOP_K, t), F32),
            jax.ShapeDtypeStruct((TOP_K, t), I32),
            jax.ShapeDtypeStruct((e, LANES), I32),
        ],
        scratch_shapes=[pltpu.VMEM((e, 1), F32), pltpu.VMEM((N_EXPERT_GROUPS, tr), F32),
                        pltpu.VMEM((N_EXPERT_GROUPS, tr), F32)],
        compiler_params=pltpu.CompilerParams(
            dimension_semantics=("arbitrary",), vmem_limit_bytes=VMEM_LIMIT_BYTES),
        name="route",
    )(logits_t, router_bias.reshape(e, 1))


def _plan_kernel(cnt_ref, idx_ref, rank_ref, dest_ref, last_ref, be_ref, nxt_ref, par_ref, nblk_ref, pst_ref, *, tb):
    e = cnt_ref.shape[0]
    nb = be_ref.shape[1]
    tr = idx_ref.shape[1]

    @pl.when(pl.program_id(0) == 0)
    def _():
        shift = tb.bit_length() - 1
        nblocks = jnp.right_shift(cnt_ref[...] + (tb - 1), shift).astype(F32)
        lower = jnp.where(lax.broadcasted_iota(I32, (e, e), 1) <= lax.broadcasted_iota(I32, (e, e), 0), 1.0, 0.0)
        bend = _dot(lower.astype(BF16), nblocks.astype(BF16))
        nblk = bend[e - 1:e, :]
        pst_ref[...] = ((bend - nblocks) * tb)[:, 0:1]
        last_ref[...] = jnp.where(nblocks > 0, bend - 1.0, -1.0).astype(I32)
        nblk_ref[...] = jnp.broadcast_to(nblk, nblk_ref.shape).astype(I32)
        blk = lax.broadcasted_iota(I32, (e, nb), 1).astype(F32)
        exp = lax.broadcasted_iota(I32, (e, nb), 0).astype(F32)
        be = jnp.minimum(jnp.sum(jnp.where(bend[:, 0:1] <= blk, 1.0, 0.0), axis=0, keepdims=True), e - 1.0)
        later = jnp.where((nblocks[:, 0:1] > 0) & (exp > be), exp, float(e))
        nxt = jnp.min(later, axis=0, keepdims=True)
        nxt = jnp.where(nxt >= e, -1.0, nxt)
        earlier = jnp.sum(jnp.where((nblocks[:, 0:1] > 0) & (exp < be), 1.0, 0.0), axis=0, keepdims=True)
        be_ref[...] = jnp.broadcast_to(be, be_ref.shape).astype(I32)
        nxt_ref[...] = jnp.broadcast_to(nxt, nxt_ref.shape).astype(I32)
        par_ref[...] = jnp.bitwise_and(jnp.broadcast_to(earlier, par_ref.shape).astype(I32), 1)

    rowi = lax.broadcasted_iota(I32, (e, tr), 0)
    pst = pst_ref[...]
    for k in range(TOP_K):
        start = jnp.sum(jnp.where(rowi == idx_ref[k:k + 1, :], pst, 0.0), axis=0, keepdims=True)
        dest_ref[k:k + 1, :] = start.astype(I32) + rank_ref[k:k + 1, :]


def _plan_call(cnt, idx, rank, nb, tb):
    e = cnt.shape[0]
    t = idx.shape[1]
    tr = PLAN_TR
    tok = lambda i: (0, i)
    const = lambda i: (0, 0)
    return pl.pallas_call(
        functools.partial(_plan_kernel, tb=tb),
        grid=(t // tr,),
        in_specs=[pl.BlockSpec((e, LANES), const), pl.BlockSpec((TOP_K, tr), tok), pl.BlockSpec((TOP_K, tr), tok)],
        out_specs=[
            pl.BlockSpec((TOP_K, tr), tok),
            pl.BlockSpec((e, LANES), const),
            pl.BlockSpec((SUBLANES, nb), const),
            pl.BlockSpec((SUBLANES, nb), const),
            pl.BlockSpec((SUBLANES, nb), const),
            pl.BlockSpec((SUBLANES, LANES), const),
        ],
        out_shape=[
            jax.ShapeDtypeStruct((TOP_K, t), I32),
            jax.ShapeDtypeStruct((e, LANES), I32),
            jax.ShapeDtypeStruct((SUBLANES, nb), I32),
            jax.ShapeDtypeStruct((SUBLANES, nb), I32),
            jax.ShapeDtypeStruct((SUBLANES, nb), I32),
            jax.ShapeDtypeStruct((SUBLANES, LANES), I32),
        ],
        scratch_shapes=[pltpu.VMEM((e, 1), F32)],
        compiler_params=pltpu.CompilerParams(dimension_semantics=("arbitrary",)),
        name="plan",
    )(cnt, idx, rank)


def _padzero_kernel(last_ref, xs_out, zbuf, sem):
    n_groups = last_ref.shape[0] // PADZERO_GROUP
    rows = zbuf.shape[0]
    zbuf[...] = jnp.zeros_like(zbuf)

    def for_group(g, action):
        for j in range(PADZERO_GROUP):
            blk = last_ref[g * PADZERO_GROUP + j]

            @pl.when(blk >= 0)
            def _():
                action(pltpu.make_async_copy(zbuf, xs_out.at[pl.ds(pl.multiple_of(blk * rows, rows), rows)],
                                             sem.at[lax.rem(g, 2)]))

    def body(g, carry):
        for_group(g, lambda cp: cp.start())

        @pl.when(g > 0)
        def _():
            for_group(g - 1, lambda cp: cp.wait())
        return carry

    lax.fori_loop(0, n_groups, body, 0)
    for_group(n_groups - 1, lambda cp: cp.wait())


def _padzero_call(last_blk, nb, tb):
    grid_spec = pltpu.PrefetchScalarGridSpec(
        num_scalar_prefetch=1,
        grid=(1,),
        in_specs=[],
        out_specs=pl.BlockSpec(memory_space=pl.ANY),
        scratch_shapes=[pltpu.VMEM((tb * SUBLANES, LANES), U32), pltpu.SemaphoreType.DMA((2,))],
    )
    assert last_blk.shape[0] % PADZERO_GROUP == 0
    return pl.pallas_call(
        _padzero_kernel,
        grid_spec=grid_spec,
        out_shape=jax.ShapeDtypeStruct((nb * tb * SUBLANES, LANES), U32),
        compiler_params=pltpu.CompilerParams(dimension_semantics=("arbitrary",)),
        name="padzero",
    )(last_blk)


def _tile_rows(r):
    return pl.ds(pl.multiple_of(r * SUBLANES, SUBLANES), SUBLANES)


def _dispatch_kernel(dest_ref, h_ref, xs_in, xs_out, sem):
    del xs_in
    tm = dest_ref.shape[1]

    def group(g, carry):
        base = pl.multiple_of(g * ROW_DMA_GROUP, ROW_DMA_GROUP)
        for j in range(ROW_DMA_GROUP):
            for k in range(TOP_K):
                pltpu.make_async_copy(h_ref.at[_tile_rows(base + j)], xs_out.at[_tile_rows(dest_ref[k, base + j])],
                                      sem).start(priority=(j * TOP_K + k) % 2)
        return carry

    lax.fori_loop(0, tm // ROW_DMA_GROUP, group, 0)
    for k in range(TOP_K):
        pltpu.make_async_copy(h_ref, xs_out.at[pl.ds(0, tm * SUBLANES)], sem).wait()


def _dispatch_call(dest, h2p, xs):
    t = dest.shape[1]
    tm = DISP_TM
    return pl.pallas_call(
        _dispatch_kernel,
        grid=(t // tm,),
        in_specs=[
            pl.BlockSpec((TOP_K, tm), lambda i: (0, i), memory_space=pltpu.SMEM),
            pl.BlockSpec((tm * SUBLANES, LANES), lambda i: (i, 0)),
            pl.BlockSpec(memory_space=pl.ANY),
        ],
        out_specs=pl.BlockSpec(memory_space=pl.ANY),
        out_shape=jax.ShapeDtypeStruct(xs.shape, xs.dtype),
        scratch_shapes=[pltpu.SemaphoreType.DMA],
        input_output_aliases={2: 0},
        compiler_params=pltpu.CompilerParams(
            dimension_semantics=("arbitrary",), vmem_limit_bytes=VMEM_LIMIT_BYTES),
        name="dispatch",
    )(dest, h2p, xs)


def _gmm_kernel(be_ref, nxt_ref, par_ref, nb_ref, x_ref, w1_hbm, w3_hbm, w2_hbm, o_ref,
                st1, st3, st2, sem, wb1, wb3, wb2):
    i = pl.program_id(0)
    half = st1.shape[1] // 2
    tb = x_ref.shape[0] // SUBLANES
    nblk = nb_ref[0]

    def weight_copies(ex, s):
        return (pltpu.make_async_copy(w1_hbm.at[ex], st1.at[s], sem.at[s, 0]),
                pltpu.make_async_copy(w3_hbm.at[ex], st3.at[s], sem.at[s, 1]),
                pltpu.make_async_copy(w2_hbm.at[ex], st2.at[s], sem.at[s, 2]))

    @pl.when(i == 0)
    def _():
        for cp in weight_copies(be_ref[0], par_ref[0]):
            cp.start(priority=WEIGHT_DMA_PRIORITY)

    active = i < nblk
    fresh = jnp.logical_or(i == 0, be_ref[i] != be_ref[jnp.maximum(i - 1, 0)])

    @pl.when(jnp.logical_and(active, fresh))
    def _():
        s = par_ref[i]
        nxt = nxt_ref[i]

        @pl.when(nxt >= 0)
        def _():
            for cp in weight_copies(nxt, 1 - s):
                cp.start(priority=WEIGHT_DMA_PRIORITY)

        for cp in weight_copies(be_ref[i], s):
            cp.wait()

        xa, xb = _unpack_bf16_pair(_tiles_to_rows(x_ref, tb))
        xs = (xa.astype(BF16), xb.astype(BF16))
        n_chunks = st1.shape[1] // CAST_ROWS
        per_half = n_chunks // 2
        rows2 = st2.shape[1] // n_chunks
        h1 = None
        h3 = None
        for c in range(n_chunks):
            rs = slice(c * CAST_ROWS, (c + 1) * CAST_ROWS)
            wb1[rs, :] = st1[s, rs, :].astype(BF16)
            wb3[rs, :] = st3[s, rs, :].astype(BF16)
            rs2 = slice(c * rows2, (c + 1) * rows2)
            wb2[rs2, :] = st2[s, rs2, :].astype(BF16)
            xc = xs[c // per_half][:, (c % per_half) * CAST_ROWS:(c % per_half + 1) * CAST_ROWS]
            p1 = _dot(xc, wb1[rs, :])
            p3 = _dot(xc, wb3[rs, :])
            h1 = p1 if h1 is None else h1 + p1
            h3 = p3 if h3 is None else h3 + p3
        y = _dot((_silu(h1) * h3).astype(BF16), wb2[...])
        _rows_to_tiles(o_ref, _pack_bf16_pair(y[:, :half], y[:, half:]))

    @pl.when(jnp.logical_and(active, jnp.logical_not(fresh)))
    def _():
        xa, xb = _unpack_bf16_pair(_tiles_to_rows(x_ref, tb))
        xa = xa.astype(BF16)
        xb = xb.astype(BF16)
        h1 = _dot(xa, wb1[:half, :]) + _dot(xb, wb1[half:, :])
        h3 = _dot(xa, wb3[:half, :]) + _dot(xb, wb3[half:, :])
        y = _dot((_silu(h1) * h3).astype(BF16), wb2[...])
        _rows_to_tiles(o_ref, _pack_bf16_pair(y[:, :half], y[:, half:]))

    @pl.when(jnp.logical_not(active))
    def _():
        o_ref[...] = jnp.zeros_like(o_ref)


def _gmm_call(block_e, next_e, parity, nblk, xs, w1, w3, w2):
    e, d, de = w1.shape
    assert d // 2 == SUBLANES * LANES
    tb = GMM_TB
    nb = block_e.shape[0]
    blk = (tb * SUBLANES, LANES)
    grid_spec = pltpu.PrefetchScalarGridSpec(
        num_scalar_prefetch=4,
        grid=(nb,),
        in_specs=[
            pl.BlockSpec(blk, lambda i, be, nx, pa, n: (jnp.minimum(i, n[0] - 1), 0)),
            pl.BlockSpec(memory_space=pl.ANY),
            pl.BlockSpec(memory_space=pl.ANY),
            pl.BlockSpec(memory_space=pl.ANY),
        ],
        out_specs=pl.BlockSpec(blk, lambda i, be, nx, pa, n: (jnp.minimum(i, n[0]), 0)),
        scratch_shapes=[
            pltpu.VMEM((2, d, de), F32),
            pltpu.VMEM((2, d, de), F32),
            pltpu.VMEM((2, de, d), F32),
            pltpu.SemaphoreType.DMA((2, 3)),
            pltpu.VMEM((d, de), BF16),
            pltpu.VMEM((d, de), BF16),
            pltpu.VMEM((de, d), BF16),
        ],
    )
    return pl.pallas_call(
        _gmm_kernel,
        grid_spec=grid_spec,
        out_shape=jax.ShapeDtypeStruct((nb * tb * SUBLANES, LANES), U32),
        compiler_params=pltpu.CompilerParams(
            dimension_semantics=("arbitrary",), vmem_limit_bytes=VMEM_LIMIT_BYTES),
        name="gmm",
    )(block_e, next_e, parity, nblk, xs, w1, w3, w2)


def _final_kernel(destc_ref, destn_ref, gt_ref, x1_ref, h2p_ref, ada_ref, ws1_ref, ws3_ref, ws2_ref,
                  g_ref, b_ref, ys_hbm, o_ref, gbuf, gb_ref, sem):
    i = pl.program_id(0)
    last = pl.num_programs(0) - 1
    tm, d = x1_ref.shape
    half = d // 2
    n_lane_tiles = d // LANES
    slot = lax.rem(i, 2)
    other = 1 - slot

    def start_rows(dest_ref, s, t, j):
        for k in range(TOP_K):
            pltpu.make_async_copy(ys_hbm.at[_tile_rows(dest_ref[k, t])], gbuf.at[s, k, _tile_rows(t)],
                                  sem.at[s, k]).start(priority=(j * TOP_K + k) % 2)

    def wait_rows(s):
        for k in range(TOP_K):
            pltpu.make_async_copy(ys_hbm.at[pl.ds(0, tm * SUBLANES)], gbuf.at[s, k], sem.at[s, k]).wait()

    @pl.when(i == 0)
    def _():
        def group(g, carry):
            base = pl.multiple_of(g * ROW_DMA_GROUP, ROW_DMA_GROUP)
            for j in range(ROW_DMA_GROUP):
                start_rows(destc_ref, 0, base + j, j)
            return carry
        lax.fori_loop(0, tm // ROW_DMA_GROUP, group, 0)

    wait_rows(slot)

    xa, xb = _unpack_bf16_pair(_tiles_to_rows(h2p_ref, tm))
    xa = xa.astype(BF16)
    xb = xb.astype(BF16)
    h1 = _dot(xa, ws1_ref[:half, :]) + _dot(xb, ws1_ref[half:, :])
    h3 = _dot(xa, ws3_ref[:half, :]) + _dot(xb, ws3_ref[half:, :])
    shared = _dot((_silu(h1) * h3).astype(BF16), ws2_ref[...])

    for k in range(TOP_K):
        gb_ref[k] = jnp.broadcast_to(gt_ref[:, k:k + 1], (tm, LANES))
    gate2 = ada_ref[0, 5:6, :]

    per_chunk = tm // SUBLANES
    total = jnp.zeros((tm, LANES), F32)
    for c in range(SUBLANES):
        ra = jnp.zeros((tm, LANES), F32)
        rb = jnp.zeros((tm, LANES), F32)
        for k in range(TOP_K):
            ya, yb = _unpack_bf16_pair(gbuf[slot, k, pl.ds(c, tm, stride=SUBLANES), :])
            ra = ra + ya * gb_ref[k]
            rb = rb + yb * gb_ref[k]
        for r, cs in ((ra, slice(c * LANES, (c + 1) * LANES)), (rb, slice(half + c * LANES, half + (c + 1) * LANES))):
            y = DEEPNORM_ALPHA * x1_ref[:, cs] + gate2[:, cs] * (shared[:, cs] + r)
            o_ref[:, cs] = y
            total = total + y
        for t in range(c * per_chunk, (c + 1) * per_chunk):
            start_rows(destn_ref, other, t, t)
    mu = jnp.sum(total, axis=-1, keepdims=True) / d
    sq = jnp.zeros((tm, LANES), F32)
    for c in range(n_lane_tiles):
        yc = o_ref[:, c * LANES:(c + 1) * LANES] - mu
        sq = sq + yc * yc
    inv = lax.rsqrt(jnp.sum(sq, axis=-1, keepdims=True) / d + LN_EPS)
    for c in range(n_lane_tiles):
        cs = slice(c * LANES, (c + 1) * LANES)
        o_ref[:, cs] = (o_ref[:, cs] - mu) * inv * g_ref[:, cs] + b_ref[:, cs]

    @pl.when(i == last)
    def _():
        wait_rows(other)


def _final_call(dest, gates_t, x1, h2p, ada3, ws1_b, ws3_b, ws2_b, ln2_g, ln2_b, ys, seq):
    t, d = x1.shape
    ds_ = ws1_b.shape[1]
    tm = FIN_TM
    n = t // tm
    per_seq = seq // tm
    const2 = lambda i: (0, 0)
    return pl.pallas_call(
        _final_kernel,
        grid=(n,),
        in_specs=[
            pl.BlockSpec((TOP_K, tm), lambda i: (0, i), memory_space=pltpu.SMEM),
            pl.BlockSpec((TOP_K, tm), lambda i: (0, jnp.minimum(i + 1, n - 1)), memory_space=pltpu.SMEM),
            pl.BlockSpec((tm, TOP_K), lambda i: (i, 0)),
            pl.BlockSpec((tm, d), lambda i: (i, 0)),
            pl.BlockSpec((tm * SUBLANES, LANES), lambda i: (i, 0)),
            pl.BlockSpec((1, 6, d), lambda i: (i // per_seq, 0, 0)),
            pl.BlockSpec((d, ds_), const2),
            pl.BlockSpec((d, ds_), const2),
            pl.BlockSpec((ds_, d), const2),
            pl.BlockSpec((1, d), const2),
            pl.BlockSpec((1, d), const2),
            pl.BlockSpec(memory_space=pl.ANY),
        ],
        out_specs=pl.BlockSpec((tm, d), lambda i: (i, 0)),
        out_shape=jax.ShapeDtypeStruct((t, d), F32),
        scratch_shapes=[
            pltpu.VMEM((2, TOP_K, tm * SUBLANES, LANES), U32),
            pltpu.VMEM((TOP_K, tm, LANES), F32),
            pltpu.SemaphoreType.DMA((2, TOP_K)),
        ],
        compiler_params=pltpu.CompilerParams(
            dimension_semantics=("arbitrary",), vmem_limit_bytes=VMEM_LIMIT_BYTES),
        name="final",
    )(dest, dest, gates_t, x1, h2p, ada3, ws1_b, ws3_b, ws2_b, ln2_g.reshape(1, d), ln2_b.reshape(1, d), ys)


def kernel(x, c, w_ada, b_ada, w_in, b_in, ln_v_g, ln_v_b, w_spatial, b_spatial, w_pool, b_pool, pool_scale, w_out, ln1_g, ln1_b, w_router, router_bias, w1, w3, w2, ws1, ws3, ws2, ln2_g, ln2_b):
    bsz, seq, d = x.shape
    t = bsz * seq
    e = w_router.shape[-1]
    assert w_ada.shape[0] == DEPTH
    l = 0

    ada3 = _ada_call(c, w_ada[l], b_ada[l]).reshape(bsz, 6, d)

    cat = _mix_call(x, ada3, w_in[l].astype(BF16), b_in[l], ln_v_g[l], ln_v_b[l], w_spatial[l],
                    b_spatial[l].T, w_pool[l].astype(BF16), b_pool[l], pool_scale[l])

    wr_t = w_router[l].T
    wr_hi = wr_t.astype(BF16)
    wr_lo = (wr_t - wr_hi.astype(F32)).astype(BF16)
    x1, h2p, logits_t = _out_call(cat.reshape(t, d), x.reshape(t, d), ada3, w_out[l].astype(BF16),
                                  ln1_g[l], ln1_b[l], wr_hi, wr_lo, seq)

    idx, gates, rank, cnt = _route_call(logits_t, router_bias[l])

    tb = GMM_TB
    nb = (t * TOP_K) // tb + e
    dest, last, be, nxt, par, nblk = _plan_call(cnt, idx, rank, nb, tb)

    xs = _dispatch_call(dest, h2p, _padzero_call(last[:, 0], nb, tb))

    ys = _gmm_call(be[0], nxt[0], par[0], nblk[0, :1], xs, w1.reshape(w1.shape[1:]),
                   w3.reshape(w3.shape[1:]), w2.reshape(w2.shape[1:]))

    out = _final_call(dest, gates.T, x1, h2p, ada3, ws1[l].astype(BF16), ws3[l].astype(BF16),
                      ws2[l].astype(BF16), ln2_g[l], ln2_b[l], ys, seq)
    return out.reshape(bsz, seq, d)
```

```python
import functools

import jax
import jax.numpy as jnp
from jax import lax
from jax.experimental import pallas as pl
from jax.experimental.pallas import tpu as pltpu

F32 = jnp.float32
BF16 = jnp.bfloat16
U32 = jnp.uint32
I32 = jnp.int32

N_HEADS_A = 8
GMLP_BLOCK = 128
CHUNK = 64
POOL_WINDOWS = (2, 4, 8, 16)
POOL_HISTORY = 16
N_EXPERT_GROUPS = 8
TOPK_GROUPS = 4
TOP_K = 8
ROUTED_SCALE = 2.5
LN_EPS = 1e-5
DEPTH = 1
DEEPNORM_ALPHA = (2.0 * DEPTH) ** 0.25

LANES = 128
SUBLANES = 8
VMEM_LIMIT_BYTES = 56 * 1024 * 1024
ADA_TN = 1024
MIX_TS = 256
OUT_TM = 512
OUT_SUB = 256
ROUTE_TR = 256
PLAN_TR = 512
GMM_TB = 256
DISP_TM = 512
FIN_TM = 256
FIN_ISSUE_CHUNKS = 4
ROW_DMA_GROUP = 16
PADZERO_GROUP = 8
WEIGHT_DMA_PRIORITY = 1
CAST_ROWS = 256


def _dot(a, b):
    return jnp.dot(a, b, preferred_element_type=F32)


def _split_bf16(a):
    hi = a.astype(BF16)
    lo = (a - hi.astype(F32)).astype(BF16)
    return hi, lo


def _gelu(x):
    return 0.5 * x * (1.0 + lax.erf(x * 0.7071067811865476))


def _silu(x):
    return x * jax.nn.sigmoid(x)


def _pack_bf16_pair(a, b):
    au = pltpu.bitcast(a.astype(BF16).astype(F32), U32)
    bu = pltpu.bitcast(b.astype(BF16).astype(F32), U32)
    return (au & jnp.uint32(0xFFFF0000)) | (bu >> 16)


def _unpack_bf16_pair(w):
    a = pltpu.bitcast(w & jnp.uint32(0xFFFF0000), F32)
    b = pltpu.bitcast(w << 16, F32)
    return a, b


def _tiles_to_rows(ref, n):
    return jnp.concatenate([ref[pl.ds(c, n, stride=SUBLANES), :] for c in range(SUBLANES)], axis=1)


def _rows_to_tiles(ref, value):
    n = value.shape[0]
    for c in range(SUBLANES):
        ref[pl.ds(c, n, stride=SUBLANES), :] = value[:, c * LANES:(c + 1) * LANES]


def _ada_kernel(c_ref, w_ref, b_ref, o_ref):
    s = _silu(c_ref[...])
    sh, sl = _split_bf16(s)
    wh, wl = _split_bf16(w_ref[...])
    o_ref[...] = _dot(sh, wh) + _dot(sl, wh) + _dot(sh, wl) + b_ref[...]


def _ada_call(c, w_ada, b_ada):
    bsz, d = c.shape
    n = w_ada.shape[1]
    return pl.pallas_call(
        _ada_kernel,
        grid=(n // ADA_TN,),
        in_specs=[
            pl.BlockSpec((bsz, d), lambda j: (0, 0)),
            pl.BlockSpec((d, ADA_TN), lambda j: (0, j)),
            pl.BlockSpec((1, ADA_TN), lambda j: (0, j)),
        ],
        out_specs=pl.BlockSpec((bsz, ADA_TN), lambda j: (0, j)),
        out_shape=jax.ShapeDtypeStruct((bsz, n), F32),
        compiler_params=pltpu.CompilerParams(
            dimension_semantics=("arbitrary",), vmem_limit_bytes=VMEM_LIMIT_BYTES),
        name="ada",
    )(c, w_ada, b_ada.reshape(1, n))


def _mix_kernel(x_ref, ada_ref, win_ref, bin_ref, lng_ref, lnb_ref, wsp_ref, bsp_ref,
                wpool_ref, bpool_ref, pscale_ref, o_ref, carry_ref, *, d_a, d_g):
    j = pl.program_id(1)
    ts = x_ref.shape[1]
    x = x_ref[0]
    shift = ada_ref[0, 0:1, :]
    scale = ada_ref[0, 1:2, :]
    h = (x * (1.0 + scale) + shift).astype(BF16)
    proj = _dot(h, win_ref[...]) + bin_ref[...]

    u = _gelu(proj[:, :d_a])
    v = _gelu(proj[:, d_a:2 * d_a])
    mu = jnp.mean(v, axis=-1, keepdims=True)
    vc = v - mu
    var = jnp.mean(vc * vc, axis=-1, keepdims=True)
    vn = (vc * lax.rsqrt(var + LN_EPS) * lng_ref[...] + lnb_ref[...]).astype(BF16)
    row = lax.broadcasted_iota(I32, (GMLP_BLOCK, GMLP_BLOCK), 0)
    col = lax.broadcasted_iota(I32, (GMLP_BLOCK, GMLP_BLOCK), 1)
    causal = (col // CHUNK) <= (row // CHUNK)
    hd = d_a // N_HEADS_A
    for head in range(N_HEADS_A):
        w = jnp.where(causal, wsp_ref[head], 0.0).astype(BF16)
        bias = bsp_ref[:, head:head + 1]
        cs = slice(head * hd, (head + 1) * hd)
        for n in range(ts // GMLP_BLOCK):
            rs = slice(n * GMLP_BLOCK, (n + 1) * GMLP_BLOCK)
            mixed = _dot(w, vn[rs, cs]) + bias
            o_ref[0, rs, cs] = (u[rs, cs] * mixed).astype(BF16)

    z = proj[:, 2 * d_a:]

    @pl.when(j == 0)
    def _():
        carry_ref[...] = jnp.zeros_like(carry_ref)

    ext = jnp.concatenate([carry_ref[...], z], axis=0)
    carry_ref[...] = z[ts - POOL_HISTORY:, :]
    tpos = j * ts + lax.broadcasted_iota(I32, (ts, 1), 0)
    for g, win in enumerate(POOL_WINDOWS):
        gs = slice(g * d_g, (g + 1) * d_g)
        s = ext[:, gs]
        sh = 1
        while sh < win:
            s = s + pltpu.roll(s, sh, axis=0)
            sh *= 2
        cnt = jnp.minimum(tpos + 1, win).astype(F32)
        pooled = s[POOL_HISTORY:, :] / cnt - z[:, gs]
        y = _dot(pooled.astype(BF16), wpool_ref[g]) + bpool_ref[g:g + 1, :]
        o_ref[0, :, d_a + g * d_g:d_a + (g + 1) * d_g] = (y * pscale_ref[:, gs]).astype(BF16)


def _mix_call(x, ada3, w_in_b, b_in, ln_v_g, ln_v_b, w_spatial, b_spatial_t, w_pool_b, b_pool, pool_scale):
    bsz, seq, d = x.shape
    n_proj = w_in_b.shape[1]
    d_a = ln_v_g.shape[0]
    d_b = pool_scale.shape[0]
    n_g, d_g, _ = w_pool_b.shape
    ts = MIX_TS
    const2 = lambda b, j: (0, 0)
    const3 = lambda b, j: (0, 0, 0)
    return pl.pallas_call(
        functools.partial(_mix_kernel, d_a=d_a, d_g=d_g),
        grid=(bsz, seq // ts),
        in_specs=[
            pl.BlockSpec((1, ts, d), lambda b, j: (b, j, 0)),
            pl.BlockSpec((1, 6, d), lambda b, j: (b, 0, 0)),
            pl.BlockSpec((d, n_proj), const2),
            pl.BlockSpec((1, n_proj), const2),
            pl.BlockSpec((1, d_a), const2),
            pl.BlockSpec((1, d_a), const2),
            pl.BlockSpec((N_HEADS_A, GMLP_BLOCK, GMLP_BLOCK), const3),
            pl.BlockSpec((GMLP_BLOCK, N_HEADS_A), const2),
            pl.BlockSpec((n_g, d_g, d_g), const3),
            pl.BlockSpec((n_g, d_g), const2),
            pl.BlockSpec((1, d_b), const2),
        ],
        out_specs=pl.BlockSpec((1, ts, d_a + d_b), lambda b, j: (b, j, 0)),
        out_shape=jax.ShapeDtypeStruct((bsz, seq, d_a + d_b), BF16),
        scratch_shapes=[pltpu.VMEM((POOL_HISTORY, d_b), F32)],
        compiler_params=pltpu.CompilerParams(
            dimension_semantics=("arbitrary", "arbitrary"), vmem_limit_bytes=VMEM_LIMIT_BYTES),
        name="mix",
    )(x, ada3, w_in_b, b_in.reshape(1, n_proj), ln_v_g.reshape(1, d_a), ln_v_b.reshape(1, d_a),
      w_spatial, b_spatial_t, w_pool_b, b_pool, pool_scale.reshape(1, d_b))


def _layer_norm_rows(y, g, b):
    mu = jnp.mean(y, axis=-1, keepdims=True)
    yc = y - mu
    var = jnp.mean(yc * yc, axis=-1, keepdims=True)
    return yc * lax.rsqrt(var + LN_EPS) * g + b


def _out_kernel(cat_ref, x_ref, ada_ref, wout_ref, g_ref, b_ref, wrh_ref, wrl_ref,
                x1_ref, h2p_ref, lgt_ref):
    tm, d = x_ref.shape
    half = d // 2
    gate1 = ada_ref[0, 2:3, :]
    nt = (((1,), (1,)), ((), ()))
    for r0 in range(0, tm, OUT_SUB):
        rows = slice(r0, r0 + OUT_SUB)
        mix = _dot(cat_ref[rows, :], wout_ref[...])
        x1 = _layer_norm_rows(DEEPNORM_ALPHA * x_ref[rows, :] + gate1 * mix, g_ref[...], b_ref[...])
        x1_ref[rows, :] = x1
        h2 = x1 * (1.0 + ada_ref[0, 4:5, :]) + ada_ref[0, 3:4, :]
        _rows_to_tiles(h2p_ref.at[pl.ds(r0 * SUBLANES, OUT_SUB * SUBLANES)],
                       _pack_bf16_pair(h2[:, :half], h2[:, half:]))
        hh, hl = _split_bf16(h2)
        wrh = wrh_ref[...]
        lgt_ref[:, rows] = (lax.dot_general(wrh, hh, nt, preferred_element_type=F32)
                            + lax.dot_general(wrh, hl, nt, preferred_element_type=F32)
                            + lax.dot_general(wrl_ref[...], hh, nt, preferred_element_type=F32))


def _out_call(cat, x2d, ada3, w_out_b, ln1_g, ln1_b, wr_hi_t, wr_lo_t, seq):
    t, d = x2d.shape
    e = wr_hi_t.shape[0]
    tm = OUT_TM
    per_seq = seq // tm
    const2 = lambda i: (0, 0)
    once = pl.Buffered(1)
    return pl.pallas_call(
        _out_kernel,
        grid=(t // tm,),
        in_specs=[
            pl.BlockSpec((tm, d), lambda i: (i, 0)),
            pl.BlockSpec((tm, d), lambda i: (i, 0)),
            pl.BlockSpec((1, 6, d), lambda i: (i // per_seq, 0, 0)),
            pl.BlockSpec((d, d), const2, pipeline_mode=once),
            pl.BlockSpec((1, d), const2),
            pl.BlockSpec((1, d), const2),
            pl.BlockSpec((e, d), const2, pipeline_mode=once),
            pl.BlockSpec((e, d), const2, pipeline_mode=once),
        ],
        out_specs=[
            pl.BlockSpec((tm, d), lambda i: (i, 0)),
            pl.BlockSpec((tm * SUBLANES, LANES), lambda i: (i, 0)),
            pl.BlockSpec((e, tm), lambda i: (0, i)),
        ],
        out_shape=[
            jax.ShapeDtypeStruct((t, d), F32),
            jax.ShapeDtypeStruct((t * SUBLANES, LANES), U32),
            jax.ShapeDtypeStruct((e, t), F32),
        ],
        compiler_params=pltpu.CompilerParams(
            dimension_semantics=("arbitrary",), vmem_limit_bytes=VMEM_LIMIT_BYTES),
        name="out",
    )(cat, x2d, ada3, w_out_b, ln1_g.reshape(1, d), ln1_b.reshape(1, d), wr_hi_t, wr_lo_t)


def _route_kernel(lg_ref, bias_ref, idx_ref, gate_ref, rank_ref, cnt_ref, run_ref, gs_ref, keep_ref):
    i = pl.program_id(0)
    e, tr = lg_ref.shape
    per_group = e // N_EXPERT_GROUPS
    neg = -jnp.inf

    @pl.when(i == 0)
    def _():
        run_ref[...] = jnp.zeros_like(run_ref)

    scores = jax.nn.sigmoid(lg_ref[...])
    sel = scores + bias_ref[...]
    rowi = lax.broadcasted_iota(I32, (e, tr), 0)

    for g in range(N_EXPERT_GROUPS):
        rs = slice(g * per_group, (g + 1) * per_group)
        blk = sel[rs]
        ri = lax.broadcasted_iota(I32, (per_group, tr), 0) + g * per_group
        m1 = jnp.max(blk, axis=0, keepdims=True)
        i1 = jnp.min(jnp.where(blk == m1, ri, e), axis=0, keepdims=True)
        m2 = jnp.max(jnp.where(ri == i1, neg, blk), axis=0, keepdims=True)
        gs_ref[g:g + 1, :] = m1 + m2
    gmat = gs_ref[...]
    gi = lax.broadcasted_iota(I32, gmat.shape, 0)
    beaten = jnp.zeros(gmat.shape, I32)
    for g in range(N_EXPERT_GROUPS):
        gj = gs_ref[g:g + 1, :]
        beats = jnp.where(gj > gmat, 1, jnp.where((gj == gmat) & (gi > g), 1, 0))
        beaten = beaten + beats
    keep_ref[...] = jnp.where(beaten < TOPK_GROUPS, 1.0, 0.0)
    ekeep = jnp.concatenate(
        [jnp.broadcast_to(keep_ref[g:g + 1, :], (per_group, tr)) for g in range(N_EXPERT_GROUPS)], axis=0)
    masked = jnp.where(ekeep > 0.5, sel, neg)

    idxs, tops = [], []
    for k in range(TOP_K):
        m = jnp.max(masked, axis=0, keepdims=True)
        ik = jnp.min(jnp.where(masked == m, rowi, e), axis=0, keepdims=True)
        hit = rowi == ik
        tops.append(jnp.sum(jnp.where(hit, scores, 0.0), axis=0, keepdims=True))
        masked = jnp.where(hit, neg, masked)
        idxs.append(ik)
        idx_ref[k:k + 1, :] = ik
    total = tops[0]
    for k in range(1, TOP_K):
        total = total + tops[k]
    for k in range(TOP_K):
        gate_ref[k:k + 1, :] = tops[k] / total * ROUTED_SCALE

    chosen = jnp.zeros((e, tr), F32)
    for ik in idxs:
        chosen = chosen + jnp.where(rowi == ik, 1.0, 0.0)
    before = lax.broadcasted_iota(I32, (tr, tr), 0) < lax.broadcasted_iota(I32, (tr, tr), 1)
    upper = jnp.where(before, 1.0, 0.0).astype(BF16)
    pos = _dot(chosen.astype(BF16), upper) + run_ref[...]
    for k, ik in enumerate(idxs):
        rank_ref[k:k + 1, :] = jnp.sum(jnp.where(rowi == ik, pos, 0.0), axis=0, keepdims=True).astype(I32)
    run = run_ref[...] + jnp.sum(chosen, axis=1, keepdims=True)
    run_ref[...] = run
    cnt_ref[...] = jnp.broadcast_to(run, cnt_ref.shape).astype(I32)


def _route_call(logits_t, router_bias):
    e, t = logits_t.shape
    tr = ROUTE_TR
    tok = lambda i: (0, i)
    return pl.pallas_call(
        _route_kernel,
        grid=(t // tr,),
        in_specs=[pl.BlockSpec((e, tr), tok), pl.BlockSpec((e, 1), lambda i: (0, 0))],
        out_specs=[
            pl.BlockSpec((TOP_K, tr), tok),
            pl.BlockSpec((TOP_K, tr), tok),
            pl.BlockSpec((TOP_K, tr), tok),
            pl.BlockSpec((e, LANES), lambda i: (0, 0)),
        ],
        out_shape=[
            jax.ShapeDtypeStruct((TOP_K, t), I32),
            jax.ShapeDtypeStruct((TOP_K, t), F32),
            jax.ShapeDtypeStruct((TOP_K, t), I32),
            jax.ShapeDtypeStruct((e, LANES), I32),
        ],
        scratch_shapes=[pltpu.VMEM((e, 1), F32), pltpu.VMEM((N_EXPERT_GROUPS, tr), F32),
                        pltpu.VMEM((N_EXPERT_GROUPS, tr), F32)],
        compiler_params=pltpu.CompilerParams(
            dimension_semantics=("arbitrary",), vmem_limit_bytes=VMEM_LIMIT_BYTES),
        name="route",
    )(logits_t, router_bias.reshape(e, 1))


def _plan_kernel(cnt_ref, idx_ref, rank_ref, dest_ref, last_ref, be_ref, nxt_ref, par_ref, nblk_ref, pst_ref, *, tb):
    e = cnt_ref.shape[0]
    nb = be_ref.shape[1]
    tr = idx_ref.shape[1]

    @pl.when(pl.program_id(0) == 0)
    def _():
        shift = tb.bit_length() - 1
        nblocks = jnp.right_shift(cnt_ref[...] + (tb - 1), shift).astype(F32)
        lower = jnp.where(lax.broadcasted_iota(I32, (e, e), 1) <= lax.broadcasted_iota(I32, (e, e), 0), 1.0, 0.0)
        bend = _dot(lower.astype(BF16), nblocks.astype(BF16))
        nblk = bend[e - 1:e, :]
        pst_ref[...] = ((bend - nblocks) * tb)[:, 0:1]
        last_ref[...] = jnp.where(nblocks > 0, bend - 1.0, -1.0).astype(I32)
        nblk_ref[...] = jnp.broadcast_to(nblk, nblk_ref.shape).astype(I32)
        blk = lax.broadcasted_iota(I32, (e, nb), 1).astype(F32)
        exp = lax.broadcasted_iota(I32, (e, nb), 0).astype(F32)
        be = jnp.minimum(jnp.sum(jnp.where(bend[:, 0:1] <= blk, 1.0, 0.0), axis=0, keepdims=True), e - 1.0)
        later = jnp.where((nblocks[:, 0:1] > 0) & (exp > be), exp, float(e))
        nxt = jnp.min(later, axis=0, keepdims=True)
        nxt = jnp.where(nxt >= e, -1.0, nxt)
        earlier = jnp.sum(jnp.where((nblocks[:, 0:1] > 0) & (exp < be), 1.0, 0.0), axis=0, keepdims=True)
        be_ref[...] = jnp.broadcast_to(be, be_ref.shape).astype(I32)
        nxt_ref[...] = jnp.broadcast_to(nxt, nxt_ref.shape).astype(I32)
        par_ref[...] = jnp.bitwise_and(jnp.broadcast_to(earlier, par_ref.shape).astype(I32), 1)

    rowi = lax.broadcasted_iota(I32, (e, tr), 0)
    pst = pst_ref[...]
    for k in range(TOP_K):
        start = jnp.sum(jnp.where(rowi == idx_ref[k:k + 1, :], pst, 0.0), axis=0, keepdims=True)
        dest_ref[k:k + 1, :] = start.astype(I32) + rank_ref[k:k + 1, :]


def _plan_call(cnt, idx, rank, nb, tb):
    e = cnt.shape[0]
    t = idx.shape[1]
    tr = PLAN_TR
    tok = lambda i: (0, i)
    const = lambda i: (0, 0)
    return pl.pallas_call(
        functools.partial(_plan_kernel, tb=tb),
        grid=(t // tr,),
        in_specs=[pl.BlockSpec((e, LANES), const), pl.BlockSpec((TOP_K, tr), tok), pl.BlockSpec((TOP_K, tr), tok)],
        out_specs=[
            pl.BlockSpec((TOP_K, tr), tok),
            pl.BlockSpec((e, LANES), const),
            pl.BlockSpec((SUBLANES, nb), const),
            pl.BlockSpec((SUBLANES, nb), const),
            pl.BlockSpec((SUBLANES, nb), const),
            pl.BlockSpec((SUBLANES, LANES), const),
        ],
        out_shape=[
            jax.ShapeDtypeStruct((TOP_K, t), I32),
            jax.ShapeDtypeStruct((e, LANES), I32),
            jax.ShapeDtypeStruct((SUBLANES, nb), I32),
            jax.ShapeDtypeStruct((SUBLANES, nb), I32),
            jax.ShapeDtypeStruct((SUBLANES, nb), I32),
            jax.ShapeDtypeStruct((SUBLANES, LANES), I32),
        ],
        scratch_shapes=[pltpu.VMEM((e, 1), F32)],
        compiler_params=pltpu.CompilerParams(dimension_semantics=("arbitrary",)),
        name="plan",
    )(cnt, idx, rank)


def _padzero_kernel(last_ref, xs_out, zbuf, sem):
    n_groups = last_ref.shape[0] // PADZERO_GROUP
    rows = zbuf.shape[0]
    zbuf[...] = jnp.zeros_like(zbuf)

    def for_group(g, action):
        for j in range(PADZERO_GROUP):
            blk = last_ref[g * PADZERO_GROUP + j]

            @pl.when(blk >= 0)
            def _():
                action(pltpu.make_async_copy(zbuf, xs_out.at[pl.ds(pl.multiple_of(blk * rows, rows), rows)],
                                             sem.at[lax.rem(g, 2)]))

    def body(g, carry):
        for_group(g, lambda cp: cp.start())

        @pl.when(g > 0)
        def _():
            for_group(g - 1, lambda cp: cp.wait())
        return carry

    lax.fori_loop(0, n_groups, body, 0)
    for_group(n_groups - 1, lambda cp: cp.wait())


def _padzero_call(last_blk, nb, tb):
    grid_spec = pltpu.PrefetchScalarGridSpec(
        num_scalar_prefetch=1,
        grid=(1,),
        in_specs=[],
        out_specs=pl.BlockSpec(memory_space=pl.ANY),
        scratch_shapes=[pltpu.VMEM((tb * SUBLANES, LANES), U32), pltpu.SemaphoreType.DMA((2,))],
    )
    assert last_blk.shape[0] % PADZERO_GROUP == 0
    return pl.pallas_call(
        _padzero_kernel,
        grid_spec=grid_spec,
        out_shape=jax.ShapeDtypeStruct((nb * tb * SUBLANES, LANES), U32),
        compiler_params=pltpu.CompilerParams(dimension_semantics=("arbitrary",)),
        name="padzero",
    )(last_blk)


def _tile_rows(r):
    return pl.ds(pl.multiple_of(r * SUBLANES, SUBLANES), SUBLANES)


def _dispatch_kernel(dest_ref, h_ref, xs_in, xs_out, sem):
    del xs_in
    tm = dest_ref.shape[1]

    def group(g, carry):
        base = pl.multiple_of(g * ROW_DMA_GROUP, ROW_DMA_GROUP)
        for j in range(ROW_DMA_GROUP):
            for k in range(TOP_K):
                pltpu.make_async_copy(h_ref.at[_tile_rows(base + j)], xs_out.at[_tile_rows(dest_ref[k, base + j])],
                                      sem).start(priority=(j * TOP_K + k) % 2)
        return carry

    lax.fori_loop(0, tm // ROW_DMA_GROUP, group, 0)
    for k in range(TOP_K):
        pltpu.make_async_copy(h_ref, xs_out.at[pl.ds(0, tm * SUBLANES)], sem).wait()


def _dispatch_call(dest, h2p, xs):
    t = dest.shape[1]
    tm = DISP_TM
    return pl.pallas_call(
        _dispatch_kernel,
        grid=(t // tm,),
        in_specs=[
            pl.BlockSpec((TOP_K, tm), lambda i: (0, i), memory_space=pltpu.SMEM),
            pl.BlockSpec((tm * SUBLANES, LANES), lambda i: (i, 0)),
            pl.BlockSpec(memory_space=pl.ANY),
        ],
        out_specs=pl.BlockSpec(memory_space=pl.ANY),
        out_shape=jax.ShapeDtypeStruct(xs.shape, xs.dtype),
        scratch_shapes=[pltpu.SemaphoreType.DMA],
        input_output_aliases={2: 0},
        compiler_params=pltpu.CompilerParams(
            dimension_semantics=("arbitrary",), vmem_limit_bytes=VMEM_LIMIT_BYTES),
        name="dispatch",
    )(dest, h2p, xs)


def _gmm_kernel(be_ref, nxt_ref, par_ref, nb_ref, x_ref, w1_hbm, w3_hbm, w2_hbm, o_ref,
                st1, st3, st2, sem, wb1, wb3, wb2):
    i = pl.program_id(0)
    half = st1.shape[1] // 2
    tb = x_ref.shape[0] // SUBLANES
    nblk = nb_ref[0]

    def weight_copies(ex, s):
        return (pltpu.make_async_copy(w1_hbm.at[ex], st1.at[s], sem.at[s, 0]),
                pltpu.make_async_copy(w3_hbm.at[ex], st3.at[s], sem.at[s, 1]),
                pltpu.make_async_copy(w2_hbm.at[ex], st2.at[s], sem.at[s, 2]))

    @pl.when(i == 0)
    def _():
        for cp in weight_copies(be_ref[0], par_ref[0]):
            cp.start(priority=WEIGHT_DMA_PRIORITY)

    active = i < nblk
    fresh = jnp.logical_or(i == 0, be_ref[i] != be_ref[jnp.maximum(i - 1, 0)])

    @pl.when(jnp.logical_and(active, fresh))
    def _():
        s = par_ref[i]
        nxt = nxt_ref[i]

        @pl.when(nxt >= 0)
        def _():
            for cp in weight_copies(nxt, 1 - s):
                cp.start(priority=WEIGHT_DMA_PRIORITY)

        for cp in weight_copies(be_ref[i], s):
            cp.wait()

        xa, xb = _unpack_bf16_pair(_tiles_to_rows(x_ref, tb))
        xs = (xa.astype(BF16), xb.astype(BF16))
        n_chunks = st1.shape[1] // CAST_ROWS
        per_half = n_chunks // 2
        rows2 = st2.shape[1] // n_chunks
        h1 = None
        h3 = None
        for c in range(n_chunks):
            rs = slice(c * CAST_ROWS, (c + 1) * CAST_ROWS)
            wb1[rs, :] = st1[s, rs, :].astype(BF16)
            wb3[rs, :] = st3[s, rs, :].astype(BF16)
            rs2 = slice(c * rows2, (c + 1) * rows2)
            wb2[rs2, :] = st2[s, rs2, :].astype(BF16)
            xc = xs[c // per_half][:, (c % per_half) * CAST_ROWS:(c % per_half + 1) * CAST_ROWS]
            p1 = _dot(xc, wb1[rs, :])
            p3 = _dot(xc, wb3[rs, :])
            h1 = p1 if h1 is None else h1 + p1
            h3 = p3 if h3 is None else h3 + p3
        y = _dot((_silu(h1) * h3).astype(BF16), wb2[...])
        _rows_to_tiles(o_ref, _pack_bf16_pair(y[:, :half], y[:, half:]))

    @pl.when(jnp.logical_and(active, jnp.logical_not(fresh)))
    def _():
        xa, xb = _unpack_bf16_pair(_tiles_to_rows(x_ref, tb))
        xa = xa.astype(BF16)
        xb = xb.astype(BF16)
        h1 = _dot(xa, wb1[:half, :]) + _dot(xb, wb1[half:, :])
        h3 = _dot(xa, wb3[:half, :]) + _dot(xb, wb3[half:, :])
        y = _dot((_silu(h1) * h3).astype(BF16), wb2[...])
        _rows_to_tiles(o_ref, _pack_bf16_pair(y[:, :half], y[:, half:]))

    @pl.when(i == nblk)
    def _():
        o_ref[...] = jnp.zeros_like(o_ref)


def _gmm_call(block_e, next_e, parity, nblk, xs, w1, w3, w2):
    e, d, de = w1.shape
    assert d // 2 == SUBLANES * LANES
    tb = GMM_TB
    nb = block_e.shape[0]
    blk = (tb * SUBLANES, LANES)
    grid_spec = pltpu.PrefetchScalarGridSpec(
        num_scalar_prefetch=4,
        grid=(nb,),
        in_specs=[
            pl.BlockSpec(blk, lambda i, be, nx, pa, n: (jnp.minimum(i, n[0] - 1), 0)),
            pl.BlockSpec(memory_space=pl.ANY),
            pl.BlockSpec(memory_space=pl.ANY),
            pl.BlockSpec(memory_space=pl.ANY),
        ],
        out_specs=pl.BlockSpec(blk, lambda i, be, nx, pa, n: (jnp.minimum(i, n[0]), 0)),
        scratch_shapes=[
            pltpu.VMEM((2, d, de), F32),
            pltpu.VMEM((2, d, de), F32),
            pltpu.VMEM((2, de, d), F32),
            pltpu.SemaphoreType.DMA((2, 3)),
            pltpu.VMEM((d, de), BF16),
            pltpu.VMEM((d, de), BF16),
            pltpu.VMEM((de, d), BF16),
        ],
    )
    return pl.pallas_call(
        _gmm_kernel,
        grid_spec=grid_spec,
        out_shape=jax.ShapeDtypeStruct((nb * tb * SUBLANES, LANES), U32),
        compiler_params=pltpu.CompilerParams(
            dimension_semantics=("arbitrary",), vmem_limit_bytes=VMEM_LIMIT_BYTES),
        name="gmm",
    )(block_e, next_e, parity, nblk, xs, w1, w3, w2)


def _final_kernel(destc_ref, destn_ref, gt_ref, x1_ref, h2p_ref, ada_ref, ws1_ref, ws3_ref, ws2_ref,
                  g_ref, b_ref, ys_hbm, o_ref, gbuf, gb_ref, sem):
    i = pl.program_id(0)
    last = pl.num_programs(0) - 1
    tm, d = x1_ref.shape
    half = d // 2
    n_lane_tiles = d // LANES
    slot = lax.rem(i, 2)
    other = 1 - slot

    def start_rows(dest_ref, s, t, j):
        for k in range(TOP_K):
            pltpu.make_async_copy(ys_hbm.at[_tile_rows(dest_ref[k, t])], gbuf.at[s, k, _tile_rows(t)],
                                  sem.at[s, k]).start(priority=(j * TOP_K + k) % 2)

    def wait_rows(s):
        for k in range(TOP_K):
            pltpu.make_async_copy(ys_hbm.at[pl.ds(0, tm * SUBLANES)], gbuf.at[s, k], sem.at[s, k]).wait()

    @pl.when(i == 0)
    def _():
        def group(g, carry):
            base = pl.multiple_of(g * ROW_DMA_GROUP, ROW_DMA_GROUP)
            for j in range(ROW_DMA_GROUP):
                start_rows(destc_ref, 0, base + j, j)
            return carry
        lax.fori_loop(0, tm // ROW_DMA_GROUP, group, 0)

    wait_rows(slot)

    xa, xb = _unpack_bf16_pair(_tiles_to_rows(h2p_ref, tm))
    xa = xa.astype(BF16)
    xb = xb.astype(BF16)
    h1 = _dot(xa, ws1_ref[:half, :]) + _dot(xb, ws1_ref[half:, :])
    h3 = _dot(xa, ws3_ref[:half, :]) + _dot(xb, ws3_ref[half:, :])
    shared = _dot((_silu(h1) * h3).astype(BF16), ws2_ref[...])

    for k in range(TOP_K):
        gb_ref[k] = jnp.broadcast_to(gt_ref[:, k:k + 1], (tm, LANES))
    gate2 = ada_ref[0, 5:6, :]

    per_chunk = tm // FIN_ISSUE_CHUNKS
    total = jnp.zeros((tm, LANES), F32)
    for c in range(SUBLANES):
        ra = jnp.zeros((tm, LANES), F32)
        rb = jnp.zeros((tm, LANES), F32)
        for k in range(TOP_K):
            ya, yb = _unpack_bf16_pair(gbuf[slot, k, pl.ds(c, tm, stride=SUBLANES), :])
            ra = ra + ya * gb_ref[k]
            rb = rb + yb * gb_ref[k]
        for r, cs in ((ra, slice(c * LANES, (c + 1) * LANES)), (rb, slice(half + c * LANES, half + (c + 1) * LANES))):
            y = DEEPNORM_ALPHA * x1_ref[:, cs] + gate2[:, cs] * (shared[:, cs] + r)
            o_ref[:, cs] = y
            total = total + y
        if c < FIN_ISSUE_CHUNKS:
            for t in range(c * per_chunk, (c + 1) * per_chunk):
                start_rows(destn_ref, other, t, t)
    mu = jnp.sum(total, axis=-1, keepdims=True) / d
    sq = jnp.zeros((tm, LANES), F32)
    for c in range(n_lane_tiles):
        yc = o_ref[:, c * LANES:(c + 1) * LANES] - mu
        sq = sq + yc * yc
    inv = lax.rsqrt(jnp.sum(sq, axis=-1, keepdims=True) / d + LN_EPS)
    for c in range(n_lane_tiles):
        cs = slice(c * LANES, (c + 1) * LANES)
        o_ref[:, cs] = (o_ref[:, cs] - mu) * inv * g_ref[:, cs] + b_ref[:, cs]

    @pl.when(i == last)
    def _():
        wait_rows(other)


def _final_call(dest, gates_t, x1, h2p, ada3, ws1_b, ws3_b, ws2_b, ln2_g, ln2_b, ys, seq):
    t, d = x1.shape
    ds_ = ws1_b.shape[1]
    tm = FIN_TM
    n = t // tm
    per_seq = seq // tm
    const2 = lambda i: (0, 0)
    return pl.pallas_call(
        _final_kernel,
        grid=(n,),
        in_specs=[
            pl.BlockSpec((TOP_K, tm), lambda i: (0, i), memory_space=pltpu.SMEM),
            pl.BlockSpec((TOP_K, tm), lambda i: (0, jnp.minimum(i + 1, n - 1)), memory_space=pltpu.SMEM),
            pl.BlockSpec((tm, TOP_K), lambda i: (i, 0)),
            pl.BlockSpec((tm, d), lambda i: (i, 0)),
            pl.BlockSpec((tm * SUBLANES, LANES), lambda i: (i, 0)),
            pl.BlockSpec((1, 6, d), lambda i: (i // per_seq, 0, 0)),
            pl.BlockSpec((d, ds_), const2),
            pl.BlockSpec((d, ds_), const2),
            pl.BlockSpec((ds_, d), const2),
            pl.BlockSpec((1, d), const2),
            pl.BlockSpec((1, d), const2),
            pl.BlockSpec(memory_space=pl.ANY),
        ],
        out_specs=pl.BlockSpec((tm, d), lambda i: (i, 0)),
        out_shape=jax.ShapeDtypeStruct((t, d), F32),
        scratch_shapes=[
            pltpu.VMEM((2, TOP_K, tm * SUBLANES, LANES), U32),
            pltpu.VMEM((TOP_K, tm, LANES), F32),
            pltpu.SemaphoreType.DMA((2, TOP_K)),
        ],
        compiler_params=pltpu.CompilerParams(
            dimension_semantics=("arbitrary",), vmem_limit_bytes=VMEM_LIMIT_BYTES),
        name="final",
    )(dest, dest, gates_t, x1, h2p, ada3, ws1_b, ws3_b, ws2_b, ln2_g.reshape(1, d), ln2_b.reshape(1, d), ys)


def kernel(x, c, w_ada, b_ada, w_in, b_in, ln_v_g, ln_v_b, w_spatial, b_spatial, w_pool, b_pool, pool_scale, w_out, ln1_g, ln1_b, w_router, router_bias, w1, w3, w2, ws1, ws3, ws2, ln2_g, ln2_b):
    bsz, seq, d = x.shape
    t = bsz * seq
    e = w_router.shape[-1]
    assert w_ada.shape[0] == DEPTH
    l = 0

    ada3 = _ada_call(c, w_ada[l], b_ada[l]).reshape(bsz, 6, d)

    cat = _mix_call(x, ada3, w_in[l].astype(BF16), b_in[l], ln_v_g[l], ln_v_b[l], w_spatial[l],
                    b_spatial[l].T, w_pool[l].astype(BF16), b_pool[l], pool_scale[l])

    wr_t = w_router[l].T
    wr_hi = wr_t.astype(BF16)
    wr_lo = (wr_t - wr_hi.astype(F32)).astype(BF16)
    x1, h2p, logits_t = _out_call(cat.reshape(t, d), x.reshape(t, d), ada3, w_out[l].astype(BF16),
                                  ln1_g[l], ln1_b[l], wr_hi, wr_lo, seq)

    idx, gates, rank, cnt = _route_call(logits_t, router_bias[l])

    tb = GMM_TB
    nb = (t * TOP_K) // tb + e
    dest, last, be, nxt, par, nblk = _plan_call(cnt, idx, rank, nb, tb)

    xs = _dispatch_call(dest, h2p, _padzero_call(last[:, 0], nb, tb))

    ys = _gmm_call(be[0], nxt[0], par[0], nblk[0, :1], xs, w1.reshape(w1.shape[1:]),
                   w3.reshape(w3.shape[1:]), w2.reshape(w2.shape[1:]))

    out = _final_call(dest, gates.T, x1, h2p, ada3, ws1[l].astype(BF16), ws3[l].astype(BF16),
                      ws2[l].astype(BF16), ln2_g[l], ln2_b[l], ys, seq)
    return out.reshape(bsz, seq, d)
```

```python
import functools

import jax
import jax.numpy as jnp
from jax import lax
from jax.experimental import pallas as pl
from jax.experimental.pallas import tpu as pltpu

F32 = jnp.float32
BF16 = jnp.bfloat16
U32 = jnp.uint32
I32 = jnp.int32

N_HEADS_A = 8
GMLP_BLOCK = 128
CHUNK = 64
POOL_WINDOWS = (2, 4, 8, 16)
POOL_HISTORY = 16
N_EXPERT_GROUPS = 8
TOPK_GROUPS = 4
TOP_K = 8
ROUTED_SCALE = 2.5
LN_EPS = 1e-5
DEPTH = 1
DEEPNORM_ALPHA = (2.0 * DEPTH) ** 0.25

LANES = 128
SUBLANES = 8
VMEM_LIMIT_BYTES = 56 * 1024 * 1024
ADA_TN = 1024
MIX_TS = 256
OUT_TM = 512
OUT_SUB = 256
ROUTE_TR = 256
PLAN_TR = 512
GMM_TB = 256
DISP_TM = 512
FIN_TM = 256
FIN_ISSUE_CHUNKS = 4
ROW_DMA_GROUP = 16
PADZERO_GROUP = 8
WEIGHT_DMA_PRIORITY = 1
CAST_ROWS = 256


def _dot(a, b):
    return jnp.dot(a, b, preferred_element_type=F32)


def _split_bf16(a):
    hi = a.astype(BF16)
    lo = (a - hi.astype(F32)).astype(BF16)
    return hi, lo


def _gelu(x):
    return 0.5 * x * (1.0 + lax.erf(x * 0.7071067811865476))


def _silu(x):
    return x * jax.nn.sigmoid(x)


def _pack_bf16_pair(a, b):
    au = pltpu.bitcast(a.astype(BF16).astype(F32), U32)
    bu = pltpu.bitcast(b.astype(BF16).astype(F32), U32)
    return (au & jnp.uint32(0xFFFF0000)) | (bu >> 16)


def _unpack_bf16_pair(w):
    a = pltpu.bitcast(w & jnp.uint32(0xFFFF0000), F32)
    b = pltpu.bitcast(w << 16, F32)
    return a, b


def _tiles_to_rows(ref, n):
    return jnp.concatenate([ref[pl.ds(c, n, stride=SUBLANES), :] for c in range(SUBLANES)], axis=1)


def _rows_to_tiles(ref, value):
    n = value.shape[0]
    for c in range(SUBLANES):
        ref[pl.ds(c, n, stride=SUBLANES), :] = value[:, c * LANES:(c + 1) * LANES]


def _ada_kernel(c_ref, w_ref, b_ref, o_ref):
    s = _silu(c_ref[...])
    sh, sl = _split_bf16(s)
    wh, wl = _split_bf16(w_ref[...])
    o_ref[...] = _dot(sh, wh) + _dot(sl, wh) + _dot(sh, wl) + b_ref[...]


def _ada_call(c, w_ada, b_ada):
    bsz, d = c.shape
    n = w_ada.shape[1]
    return pl.pallas_call(
        _ada_kernel,
        grid=(n // ADA_TN,),
        in_specs=[
            pl.BlockSpec((bsz, d), lambda j: (0, 0)),
            pl.BlockSpec((d, ADA_TN), lambda j: (0, j)),
            pl.BlockSpec((1, ADA_TN), lambda j: (0, j)),
        ],
        out_specs=pl.BlockSpec((bsz, ADA_TN), lambda j: (0, j)),
        out_shape=jax.ShapeDtypeStruct((bsz, n), F32),
        compiler_params=pltpu.CompilerParams(
            dimension_semantics=("arbitrary",), vmem_limit_bytes=VMEM_LIMIT_BYTES),
        name="ada",
    )(c, w_ada, b_ada.reshape(1, n))


def _mix_kernel(x_ref, ada_ref, win_ref, bin_ref, lng_ref, lnb_ref, wsp_ref, bsp_ref,
                wpool_ref, bpool_ref, pscale_ref, o_ref, carry_ref, *, d_a, d_g):
    j = pl.program_id(1)
    ts = x_ref.shape[1]
    x = x_ref[0]
    shift = ada_ref[0, 0:1, :]
    scale = ada_ref[0, 1:2, :]
    h = (x * (1.0 + scale) + shift).astype(BF16)
    proj = _dot(h, win_ref[...]) + bin_ref[...]

    u = _gelu(proj[:, :d_a])
    v = _gelu(proj[:, d_a:2 * d_a])
    mu = jnp.mean(v, axis=-1, keepdims=True)
    vc = v - mu
    var = jnp.mean(vc * vc, axis=-1, keepdims=True)
    vn = (vc * lax.rsqrt(var + LN_EPS) * lng_ref[...] + lnb_ref[...]).astype(BF16)
    row = lax.broadcasted_iota(I32, (GMLP_BLOCK, GMLP_BLOCK), 0)
    col = lax.broadcasted_iota(I32, (GMLP_BLOCK, GMLP_BLOCK), 1)
    causal = (col // CHUNK) <= (row // CHUNK)
    hd = d_a // N_HEADS_A
    for head in range(N_HEADS_A):
        w = jnp.where(causal, wsp_ref[head], 0.0).astype(BF16)
        bias = bsp_ref[:, head:head + 1]
        cs = slice(head * hd, (head + 1) * hd)
        for n in range(ts // GMLP_BLOCK):
            rs = slice(n * GMLP_BLOCK, (n + 1) * GMLP_BLOCK)
            mixed = _dot(w, vn[rs, cs]) + bias
            o_ref[0, rs, cs] = (u[rs, cs] * mixed).astype(BF16)

    z = proj[:, 2 * d_a:]

    @pl.when(j == 0)
    def _():
        carry_ref[...] = jnp.zeros_like(carry_ref)

    ext = jnp.concatenate([carry_ref[...], z], axis=0)
    carry_ref[...] = z[ts - POOL_HISTORY:, :]
    tpos = j * ts + lax.broadcasted_iota(I32, (ts, 1), 0)
    for g, win in enumerate(POOL_WINDOWS):
        gs = slice(g * d_g, (g + 1) * d_g)
        s = ext[:, gs]
        sh = 1
        while sh < win:
            s = s + pltpu.roll(s, sh, axis=0)
            sh *= 2
        cnt = jnp.minimum(tpos + 1, win).astype(F32)
        pooled = s[POOL_HISTORY:, :] / cnt - z[:, gs]
        y = _dot(pooled.astype(BF16), wpool_ref[g]) + bpool_ref[g:g + 1, :]
        o_ref[0, :, d_a + g * d_g:d_a + (g + 1) * d_g] = (y * pscale_ref[:, gs]).astype(BF16)


def _mix_call(x, ada3, w_in_b, b_in, ln_v_g, ln_v_b, w_spatial, b_spatial_t, w_pool_b, b_pool, pool_scale):
    bsz, seq, d = x.shape
    n_proj = w_in_b.shape[1]
    d_a = ln_v_g.shape[0]
    d_b = pool_scale.shape[0]
    n_g, d_g, _ = w_pool_b.shape
    ts = MIX_TS
    const2 = lambda b, j: (0, 0)
    const3 = lambda b, j: (0, 0, 0)
    return pl.pallas_call(
        functools.partial(_mix_kernel, d_a=d_a, d_g=d_g),
        grid=(bsz, seq // ts),
        in_specs=[
            pl.BlockSpec((1, ts, d), lambda b, j: (b, j, 0)),
            pl.BlockSpec((1, 6, d), lambda b, j: (b, 0, 0)),
            pl.BlockSpec((d, n_proj), const2),
            pl.BlockSpec((1, n_proj), const2),
            pl.BlockSpec((1, d_a), const2),
            pl.BlockSpec((1, d_a), const2),
            pl.BlockSpec((N_HEADS_A, GMLP_BLOCK, GMLP_BLOCK), const3),
            pl.BlockSpec((GMLP_BLOCK, N_HEADS_A), const2),
            pl.BlockSpec((n_g, d_g, d_g), const3),
            pl.BlockSpec((n_g, d_g), const2),
            pl.BlockSpec((1, d_b), const2),
        ],
        out_specs=pl.BlockSpec((1, ts, d_a + d_b), lambda b, j: (b, j, 0)),
        out_shape=jax.ShapeDtypeStruct((bsz, seq, d_a + d_b), BF16),
        scratch_shapes=[pltpu.VMEM((POOL_HISTORY, d_b), F32)],
        compiler_params=pltpu.CompilerParams(
            dimension_semantics=("arbitrary", "arbitrary"), vmem_limit_bytes=VMEM_LIMIT_BYTES),
        name="mix",
    )(x, ada3, w_in_b, b_in.reshape(1, n_proj), ln_v_g.reshape(1, d_a), ln_v_b.reshape(1, d_a),
      w_spatial, b_spatial_t, w_pool_b, b_pool, pool_scale.reshape(1, d_b))


def _layer_norm_rows(y, g, b):
    mu = jnp.mean(y, axis=-1, keepdims=True)
    yc = y - mu
    var = jnp.mean(yc * yc, axis=-1, keepdims=True)
    return yc * lax.rsqrt(var + LN_EPS) * g + b


def _out_kernel(cat_ref, x_ref, ada_ref, wout_ref, g_ref, b_ref, wrh_ref, wrl_ref,
                x1_ref, h2p_ref, lgt_ref):
    tm, d = x_ref.shape
    half = d // 2
    gate1 = ada_ref[0, 2:3, :]
    nt = (((1,), (1,)), ((), ()))
    for r0 in range(0, tm, OUT_SUB):
        rows = slice(r0, r0 + OUT_SUB)
        mix = _dot(cat_ref[rows, :], wout_ref[...])
        x1 = _layer_norm_rows(DEEPNORM_ALPHA * x_ref[rows, :] + gate1 * mix, g_ref[...], b_ref[...])
        x1_ref[rows, :] = x1
        h2 = x1 * (1.0 + ada_ref[0, 4:5, :]) + ada_ref[0, 3:4, :]
        _rows_to_tiles(h2p_ref.at[pl.ds(r0 * SUBLANES, OUT_SUB * SUBLANES)],
                       _pack_bf16_pair(h2[:, :half], h2[:, half:]))
        hh, hl = _split_bf16(h2)
        wrh = wrh_ref[...]
        lgt_ref[:, rows] = (lax.dot_general(wrh, hh, nt, preferred_element_type=F32)
                            + lax.dot_general(wrh, hl, nt, preferred_element_type=F32)
                            + lax.dot_general(wrl_ref[...], hh, nt, preferred_element_type=F32))


def _out_call(cat, x2d, ada3, w_out_b, ln1_g, ln1_b, wr_hi_t, wr_lo_t, seq):
    t, d = x2d.shape
    e = wr_hi_t.shape[0]
    tm = OUT_TM
    per_seq = seq // tm
    const2 = lambda i: (0, 0)
    once = pl.Buffered(1)
    return pl.pallas_call(
        _out_kernel,
        grid=(t // tm,),
        in_specs=[
            pl.BlockSpec((tm, d), lambda i: (i, 0)),
            pl.BlockSpec((tm, d), lambda i: (i, 0)),
            pl.BlockSpec((1, 6, d), lambda i: (i // per_seq, 0, 0)),
            pl.BlockSpec((d, d), const2, pipeline_mode=once),
            pl.BlockSpec((1, d), const2),
            pl.BlockSpec((1, d), const2),
            pl.BlockSpec((e, d), const2, pipeline_mode=once),
            pl.BlockSpec((e, d), const2, pipeline_mode=once),
        ],
        out_specs=[
            pl.BlockSpec((tm, d), lambda i: (i, 0)),
            pl.BlockSpec((tm * SUBLANES, LANES), lambda i: (i, 0)),
            pl.BlockSpec((e, tm), lambda i: (0, i)),
        ],
        out_shape=[
            jax.ShapeDtypeStruct((t, d), F32),
            jax.ShapeDtypeStruct((t * SUBLANES, LANES), U32),
            jax.ShapeDtypeStruct((e, t), F32),
        ],
        compiler_params=pltpu.CompilerParams(
            dimension_semantics=("arbitrary",), vmem_limit_bytes=VMEM_LIMIT_BYTES),
        name="out",
    )(cat, x2d, ada3, w_out_b, ln1_g.reshape(1, d), ln1_b.reshape(1, d), wr_hi_t, wr_lo_t)


def _route_kernel(lg_ref, bias_ref, idx_ref, gate_ref, rank_ref, cnt_ref, run_ref, gs_ref, keep_ref):
    i = pl.program_id(0)
    e, tr = lg_ref.shape
    per_group = e // N_EXPERT_GROUPS
    neg = -jnp.inf

    @pl.when(i == 0)
    def _():
        run_ref[...] = jnp.zeros_like(run_ref)

    scores = jax.nn.sigmoid(lg_ref[...])
    sel = scores + bias_ref[...]
    rowi = lax.broadcasted_iota(I32, (e, tr), 0)

    for g in range(N_EXPERT_GROUPS):
        rs = slice(g * per_group, (g + 1) * per_group)
        blk = sel[rs]
        ri = lax.broadcasted_iota(I32, (per_group, tr), 0) + g * per_group
        m1 = jnp.max(blk, axis=0, keepdims=True)
        i1 = jnp.min(jnp.where(blk == m1, ri, e), axis=0, keepdims=True)
        m2 = jnp.max(jnp.where(ri == i1, neg, blk), axis=0, keepdims=True)
        gs_ref[g:g + 1, :] = m1 + m2
    gmat = gs_ref[...]
    gi = lax.broadcasted_iota(I32, gmat.shape, 0)
    beaten = jnp.zeros(gmat.shape, I32)
    for g in range(N_EXPERT_GROUPS):
        gj = gs_ref[g:g + 1, :]
        beats = jnp.where(gj > gmat, 1, jnp.where((gj == gmat) & (gi > g), 1, 0))
        beaten = beaten + beats
    keep_ref[...] = jnp.where(beaten < TOPK_GROUPS, 1.0, 0.0)
    ekeep = jnp.concatenate(
        [jnp.broadcast_to(keep_ref[g:g + 1, :], (per_group, tr)) for g in range(N_EXPERT_GROUPS)], axis=0)
    masked = jnp.where(ekeep > 0.5, sel, neg)

    idxs, tops = [], []
    for k in range(TOP_K):
        m = jnp.max(masked, axis=0, keepdims=True)
        ik = jnp.min(jnp.where(masked == m, rowi, e), axis=0, keepdims=True)
        hit = rowi == ik
        tops.append(jnp.sum(jnp.where(hit, scores, 0.0), axis=0, keepdims=True))
        masked = jnp.where(hit, neg, masked)
        idxs.append(ik)
        idx_ref[k:k + 1, :] = ik
    total = tops[0]
    for k in range(1, TOP_K):
        total = total + tops[k]
    for k in range(TOP_K):
        gate_ref[k:k + 1, :] = tops[k] / total * ROUTED_SCALE

    chosen = jnp.zeros((e, tr), F32)
    for ik in idxs:
        chosen = chosen + jnp.where(rowi == ik, 1.0, 0.0)
    before = lax.broadcasted_iota(I32, (tr, tr), 0) < lax.broadcasted_iota(I32, (tr, tr), 1)
    upper = jnp.where(before, 1.0, 0.0).astype(BF16)
    pos = _dot(chosen.astype(BF16), upper) + run_ref[...]
    for k, ik in enumerate(idxs):
        rank_ref[k:k + 1, :] = jnp.sum(jnp.where(rowi == ik, pos, 0.0), axis=0, keepdims=True).astype(I32)
    run = run_ref[...] + jnp.sum(chosen, axis=1, keepdims=True)
    run_ref[...] = run
    cnt_ref[...] = jnp.broadcast_to(run, cnt_ref.shape).astype(I32)


def _route_call(logits_t, router_bias):
    e, t = logits_t.shape
    tr = ROUTE_TR
    tok = lambda i: (0, i)
    return pl.pallas_call(
        _route_kernel,
        grid=(t // tr,),
        in_specs=[pl.BlockSpec((e, tr), tok), pl.BlockSpec((e, 1), lambda i: (0, 0))],
        out_specs=[
            pl.BlockSpec((TOP_K, tr), tok),
            pl.BlockSpec((TOP_K, tr), tok),
            pl.BlockSpec((TOP_K, tr), tok),
            pl.BlockSpec((e, LANES), lambda i: (0, 0)),
        ],
        out_shape=[
            jax.ShapeDtypeStruct((TOP_K, t), I32),
            jax.ShapeDtypeStruct((TOP_K, t), F32),
            jax.ShapeDtypeStruct((TOP_K, t), I32),
            jax.ShapeDtypeStruct((e, LANES), I32),
        ],
        scratch_shapes=[pltpu.VMEM((e, 1), F32), pltpu.VMEM((N_EXPERT_GROUPS, tr), F32),
                        pltpu.VMEM((N_EXPERT_GROUPS, tr), F32)],
        compiler_params=pltpu.CompilerParams(
            dimension_semantics=("arbitrary",), vmem_limit_bytes=VMEM_LIMIT_BYTES),
        name="route",
    )(logits_t, router_bias.reshape(e, 1))


def _plan_kernel(cnt_ref, idx_ref, rank_ref, dest_ref, last_ref, be_ref, nxt_ref, nxt2_ref, par_ref, nblk_ref,
                 pst_ref, *, tb):
    e = cnt_ref.shape[0]
    nb = be_ref.shape[1]
    tr = idx_ref.shape[1]

    @pl.when(pl.program_id(0) == 0)
    def _():
        shift = tb.bit_length() - 1
        nblocks = jnp.right_shift(cnt_ref[...] + (tb - 1), shift).astype(F32)
        lower = jnp.where(lax.broadcasted_iota(I32, (e, e), 1) <= lax.broadcasted_iota(I32, (e, e), 0), 1.0, 0.0)
        bend = _dot(lower.astype(BF16), nblocks.astype(BF16))
        nblk = bend[e - 1:e, :]
        pst_ref[...] = ((bend - nblocks) * tb)[:, 0:1]
        last_ref[...] = jnp.where(nblocks > 0, bend - 1.0, -1.0).astype(I32)
        nblk_ref[...] = jnp.broadcast_to(nblk, nblk_ref.shape).astype(I32)
        blk = lax.broadcasted_iota(I32, (e, nb), 1).astype(F32)
        exp = lax.broadcasted_iota(I32, (e, nb), 0).astype(F32)
        be = jnp.minimum(jnp.sum(jnp.where(bend[:, 0:1] <= blk, 1.0, 0.0), axis=0, keepdims=True), e - 1.0)
        later = jnp.where((nblocks[:, 0:1] > 0) & (exp > be), exp, float(e))
        nxt = jnp.min(later, axis=0, keepdims=True)
        later2 = jnp.where((nblocks[:, 0:1] > 0) & (exp > nxt), exp, float(e))
        nxt2 = jnp.min(later2, axis=0, keepdims=True)
        nxt2 = jnp.where(nxt2 >= e, -1.0, nxt2)
        nxt = jnp.where(nxt >= e, -1.0, nxt)
        earlier = jnp.sum(jnp.where((nblocks[:, 0:1] > 0) & (exp < be), 1.0, 0.0), axis=0, keepdims=True)
        be_ref[...] = jnp.broadcast_to(be, be_ref.shape).astype(I32)
        nxt_ref[...] = jnp.broadcast_to(nxt, nxt_ref.shape).astype(I32)
        nxt2_ref[...] = jnp.broadcast_to(nxt2, nxt2_ref.shape).astype(I32)
        par_ref[...] = jnp.bitwise_and(jnp.broadcast_to(earlier, par_ref.shape).astype(I32), 1)

    rowi = lax.broadcasted_iota(I32, (e, tr), 0)
    pst = pst_ref[...]
    for k in range(TOP_K):
        start = jnp.sum(jnp.where(rowi == idx_ref[k:k + 1, :], pst, 0.0), axis=0, keepdims=True)
        dest_ref[k:k + 1, :] = start.astype(I32) + rank_ref[k:k + 1, :]


def _plan_call(cnt, idx, rank, nb, tb):
    e = cnt.shape[0]
    t = idx.shape[1]
    tr = PLAN_TR
    tok = lambda i: (0, i)
    const = lambda i: (0, 0)
    return pl.pallas_call(
        functools.partial(_plan_kernel, tb=tb),
        grid=(t // tr,),
        in_specs=[pl.BlockSpec((e, LANES), const), pl.BlockSpec((TOP_K, tr), tok), pl.BlockSpec((TOP_K, tr), tok)],
        out_specs=[
            pl.BlockSpec((TOP_K, tr), tok),
            pl.BlockSpec((e, LANES), const),
            pl.BlockSpec((SUBLANES, nb), const),
            pl.BlockSpec((SUBLANES, nb), const),
            pl.BlockSpec((SUBLANES, nb), const),
            pl.BlockSpec((SUBLANES, nb), const),
            pl.BlockSpec((SUBLANES, LANES), const),
        ],
        out_shape=[
            jax.ShapeDtypeStruct((TOP_K, t), I32),
            jax.ShapeDtypeStruct((e, LANES), I32),
            jax.ShapeDtypeStruct((SUBLANES, nb), I32),
            jax.ShapeDtypeStruct((SUBLANES, nb), I32),
            jax.ShapeDtypeStruct((SUBLANES, nb), I32),
            jax.ShapeDtypeStruct((SUBLANES, nb), I32),
            jax.ShapeDtypeStruct((SUBLANES, LANES), I32),
        ],
        scratch_shapes=[pltpu.VMEM((e, 1), F32)],
        compiler_params=pltpu.CompilerParams(dimension_semantics=("arbitrary",)),
        name="plan",
    )(cnt, idx, rank)


def _padzero_kernel(last_ref, xs_out, zbuf, sem):
    n_groups = last_ref.shape[0] // PADZERO_GROUP
    rows = zbuf.shape[0]
    zbuf[...] = jnp.zeros_like(zbuf)

    def for_group(g, action):
        for j in range(PADZERO_GROUP):
            blk = last_ref[g * PADZERO_GROUP + j]

            @pl.when(blk >= 0)
            def _():
                action(pltpu.make_async_copy(zbuf, xs_out.at[pl.ds(pl.multiple_of(blk * rows, rows), rows)],
                                             sem.at[lax.rem(g, 2)]))

    def body(g, carry):
        for_group(g, lambda cp: cp.start())

        @pl.when(g > 0)
        def _():
            for_group(g - 1, lambda cp: cp.wait())
        return carry

    lax.fori_loop(0, n_groups, body, 0)
    for_group(n_groups - 1, lambda cp: cp.wait())


def _padzero_call(last_blk, nb, tb):
    grid_spec = pltpu.PrefetchScalarGridSpec(
        num_scalar_prefetch=1,
        grid=(1,),
        in_specs=[],
        out_specs=pl.BlockSpec(memory_space=pl.ANY),
        scratch_shapes=[pltpu.VMEM((tb * SUBLANES, LANES), U32), pltpu.SemaphoreType.DMA((2,))],
    )
    assert last_blk.shape[0] % PADZERO_GROUP == 0
    return pl.pallas_call(
        _padzero_kernel,
        grid_spec=grid_spec,
        out_shape=jax.ShapeDtypeStruct((nb * tb * SUBLANES, LANES), U32),
        compiler_params=pltpu.CompilerParams(dimension_semantics=("arbitrary",)),
        name="padzero",
    )(last_blk)


def _tile_rows(r):
    return pl.ds(pl.multiple_of(r * SUBLANES, SUBLANES), SUBLANES)


def _dispatch_kernel(dest_ref, h_ref, xs_in, xs_out, sem):
    del xs_in
    tm = dest_ref.shape[1]

    def group(g, carry):
        base = pl.multiple_of(g * ROW_DMA_GROUP, ROW_DMA_GROUP)
        for j in range(ROW_DMA_GROUP):
            for k in range(TOP_K):
                pltpu.make_async_copy(h_ref.at[_tile_rows(base + j)], xs_out.at[_tile_rows(dest_ref[k, base + j])],
                                      sem).start(priority=(j * TOP_K + k) % 2)
        return carry

    lax.fori_loop(0, tm // ROW_DMA_GROUP, group, 0)
    for k in range(TOP_K):
        pltpu.make_async_copy(h_ref, xs_out.at[pl.ds(0, tm * SUBLANES)], sem).wait()


def _dispatch_call(dest, h2p, xs):
    t = dest.shape[1]
    tm = DISP_TM
    return pl.pallas_call(
        _dispatch_kernel,
        grid=(t // tm,),
        in_specs=[
            pl.BlockSpec((TOP_K, tm), lambda i: (0, i), memory_space=pltpu.SMEM),
            pl.BlockSpec((tm * SUBLANES, LANES), lambda i: (i, 0)),
            pl.BlockSpec(memory_space=pl.ANY),
        ],
        out_specs=pl.BlockSpec(memory_space=pl.ANY),
        out_shape=jax.ShapeDtypeStruct(xs.shape, xs.dtype),
        scratch_shapes=[pltpu.SemaphoreType.DMA],
        input_output_aliases={2: 0},
        compiler_params=pltpu.CompilerParams(
            dimension_semantics=("arbitrary",), vmem_limit_bytes=VMEM_LIMIT_BYTES),
        name="dispatch",
    )(dest, h2p, xs)


def _gmm_kernel(be_ref, nxt_ref, nxt2_ref, par_ref, nb_ref, x_ref, w1_hbm, w3_hbm, w2_hbm, o_ref,
                st1, st3, st2, sem, wb1, wb3, wb2):
    i = pl.program_id(0)
    half = st1.shape[1] // 2
    tb = x_ref.shape[0] // SUBLANES
    nblk = nb_ref[0]

    def weight_copies(ex, s):
        return (pltpu.make_async_copy(w1_hbm.at[ex], st1.at[s], sem.at[s, 0]),
                pltpu.make_async_copy(w3_hbm.at[ex], st3.at[s], sem.at[s, 1]),
                pltpu.make_async_copy(w2_hbm.at[ex], st2.at[s], sem.at[s, 2]))

    def start_weights(ex, s):
        @pl.when(ex >= 0)
        def _():
            for cp in weight_copies(ex, s):
                cp.start(priority=WEIGHT_DMA_PRIORITY)

    @pl.when(i == 0)
    def _():
        start_weights(be_ref[0], par_ref[0])
        start_weights(nxt_ref[0], 1 - par_ref[0])

    prev = jnp.maximum(i - 1, 0)
    prev_fresh = jnp.logical_or(i == 1, be_ref[prev] != be_ref[jnp.maximum(i - 2, 0)])

    @pl.when(jnp.logical_and(jnp.logical_and(i >= 1, i - 1 < nblk), prev_fresh))
    def _():
        start_weights(nxt2_ref[prev], par_ref[prev])

    active = i < nblk
    fresh = jnp.logical_or(i == 0, be_ref[i] != be_ref[jnp.maximum(i - 1, 0)])

    @pl.when(jnp.logical_and(active, fresh))
    def _():
        s = par_ref[i]
        for cp in weight_copies(be_ref[i], s):
            cp.wait()

        xa, xb = _unpack_bf16_pair(_tiles_to_rows(x_ref, tb))
        xs = (xa.astype(BF16), xb.astype(BF16))
        n_chunks = st1.shape[1] // CAST_ROWS
        per_half = n_chunks // 2
        rows2 = st2.shape[1] // n_chunks
        h1 = None
        h3 = None
        for c in range(n_chunks):
            rs = slice(c * CAST_ROWS, (c + 1) * CAST_ROWS)
            wb1[rs, :] = st1[s, rs, :].astype(BF16)
            wb3[rs, :] = st3[s, rs, :].astype(BF16)
            rs2 = slice(c * rows2, (c + 1) * rows2)
            wb2[rs2, :] = st2[s, rs2, :].astype(BF16)
            xc = xs[c // per_half][:, (c % per_half) * CAST_ROWS:(c % per_half + 1) * CAST_ROWS]
            p1 = _dot(xc, wb1[rs, :])
            p3 = _dot(xc, wb3[rs, :])
            h1 = p1 if h1 is None else h1 + p1
            h3 = p3 if h3 is None else h3 + p3
        y = _dot((_silu(h1) * h3).astype(BF16), wb2[...])
        _rows_to_tiles(o_ref, _pack_bf16_pair(y[:, :half], y[:, half:]))

    @pl.when(jnp.logical_and(active, jnp.logical_not(fresh)))
    def _():
        xa, xb = _unpack_bf16_pair(_tiles_to_rows(x_ref, tb))
        xa = xa.astype(BF16)
        xb = xb.astype(BF16)
        h1 = _dot(xa, wb1[:half, :]) + _dot(xb, wb1[half:, :])
        h3 = _dot(xa, wb3[:half, :]) + _dot(xb, wb3[half:, :])
        y = _dot((_silu(h1) * h3).astype(BF16), wb2[...])
        _rows_to_tiles(o_ref, _pack_bf16_pair(y[:, :half], y[:, half:]))

    @pl.when(i == nblk)
    def _():
        o_ref[...] = jnp.zeros_like(o_ref)


def _gmm_call(block_e, next_e, next2_e, parity, nblk, xs, w1, w3, w2):
    e, d, de = w1.shape
    assert d // 2 == SUBLANES * LANES
    tb = GMM_TB
    nb = block_e.shape[0]
    blk = (tb * SUBLANES, LANES)
    grid_spec = pltpu.PrefetchScalarGridSpec(
        num_scalar_prefetch=5,
        grid=(nb,),
        in_specs=[
            pl.BlockSpec(blk, lambda i, be, nx, nx2, pa, n: (jnp.minimum(i, n[0] - 1), 0)),
            pl.BlockSpec(memory_space=pl.ANY),
            pl.BlockSpec(memory_space=pl.ANY),
            pl.BlockSpec(memory_space=pl.ANY),
        ],
        out_specs=pl.BlockSpec(blk, lambda i, be, nx, nx2, pa, n: (jnp.minimum(i, n[0]), 0)),
        scratch_shapes=[
            pltpu.VMEM((2, d, de), F32),
            pltpu.VMEM((2, d, de), F32),
            pltpu.VMEM((2, de, d), F32),
            pltpu.SemaphoreType.DMA((2, 3)),
            pltpu.VMEM((d, de), BF16),
            pltpu.VMEM((d, de), BF16),
            pltpu.VMEM((de, d), BF16),
        ],
    )
    return pl.pallas_call(
        _gmm_kernel,
        grid_spec=grid_spec,
        out_shape=jax.ShapeDtypeStruct((nb * tb * SUBLANES, LANES), U32),
        compiler_params=pltpu.CompilerParams(
            dimension_semantics=("arbitrary",), vmem_limit_bytes=VMEM_LIMIT_BYTES),
        name="gmm",
    )(block_e, next_e, next2_e, parity, nblk, xs, w1, w3, w2)


def _final_kernel(destc_ref, destn_ref, gt_ref, x1_ref, h2p_ref, ada_ref, ws1_ref, ws3_ref, ws2_ref,
                  g_ref, b_ref, ys_hbm, o_ref, gbuf, gb_ref, sem):
    i = pl.program_id(0)
    last = pl.num_programs(0) - 1
    tm, d = x1_ref.shape
    half = d // 2
    n_lane_tiles = d // LANES
    slot = lax.rem(i, 2)
    other = 1 - slot

    def start_rows(dest_ref, s, t, j):
        for k in range(TOP_K):
            pltpu.make_async_copy(ys_hbm.at[_tile_rows(dest_ref[k, t])], gbuf.at[s, k, _tile_rows(t)],
                                  sem.at[s, k]).start(priority=(j * TOP_K + k) % 2)

    def wait_rows(s):
        for k in range(TOP_K):
            pltpu.make_async_copy(ys_hbm.at[pl.ds(0, tm * SUBLANES)], gbuf.at[s, k], sem.at[s, k]).wait()

    @pl.when(i == 0)
    def _():
        def group(g, carry):
            base = pl.multiple_of(g * ROW_DMA_GROUP, ROW_DMA_GROUP)
            for j in range(ROW_DMA_GROUP):
                start_rows(destc_ref, 0, base + j, j)
            return carry
        lax.fori_loop(0, tm // ROW_DMA_GROUP, group, 0)

    wait_rows(slot)

    xa, xb = _unpack_bf16_pair(_tiles_to_rows(h2p_ref, tm))
    xa = xa.astype(BF16)
    xb = xb.astype(BF16)
    h1 = _dot(xa, ws1_ref[:half, :]) + _dot(xb, ws1_ref[half:, :])
    h3 = _dot(xa, ws3_ref[:half, :]) + _dot(xb, ws3_ref[half:, :])
    shared = _dot((_silu(h1) * h3).astype(BF16), ws2_ref[...])

    for k in range(TOP_K):
        gb_ref[k] = jnp.broadcast_to(gt_ref[:, k:k + 1], (tm, LANES))
    gate2 = ada_ref[0, 5:6, :]

    per_chunk = tm // FIN_ISSUE_CHUNKS
    total = jnp.zeros((tm, LANES), F32)
    for c in range(SUBLANES):
        ra = jnp.zeros((tm, LANES), F32)
        rb = jnp.zeros((tm, LANES), F32)
        for k in range(TOP_K):
            ya, yb = _unpack_bf16_pair(gbuf[slot, k, pl.ds(c, tm, stride=SUBLANES), :])
            ra = ra + ya * gb_ref[k]
            rb = rb + yb * gb_ref[k]
        for r, cs in ((ra, slice(c * LANES, (c + 1) * LANES)), (rb, slice(half + c * LANES, half + (c + 1) * LANES))):
            y = DEEPNORM_ALPHA * x1_ref[:, cs] + gate2[:, cs] * (shared[:, cs] + r)
            o_ref[:, cs] = y
            total = total + y
        if c < FIN_ISSUE_CHUNKS:
            for t in range(c * per_chunk, (c + 1) * per_chunk):
                start_rows(destn_ref, other, t, t)
    mu = jnp.sum(total, axis=-1, keepdims=True) / d
    sq = jnp.zeros((tm, LANES), F32)
    for c in range(n_lane_tiles):
        yc = o_ref[:, c * LANES:(c + 1) * LANES] - mu
        sq = sq + yc * yc
    inv = lax.rsqrt(jnp.sum(sq, axis=-1, keepdims=True) / d + LN_EPS)
    for c in range(n_lane_tiles):
        cs = slice(c * LANES, (c + 1) * LANES)
        o_ref[:, cs] = (o_ref[:, cs] - mu) * inv * g_ref[:, cs] + b_ref[:, cs]

    @pl.when(i == last)
    def _():
        wait_rows(other)


def _final_call(dest, gates_t, x1, h2p, ada3, ws1_b, ws3_b, ws2_b, ln2_g, ln2_b, ys, seq):
    t, d = x1.shape
    ds_ = ws1_b.shape[1]
    tm = FIN_TM
    n = t // tm
    per_seq = seq // tm
    const2 = lambda i: (0, 0)
    return pl.pallas_call(
        _final_kernel,
        grid=(n,),
        in_specs=[
            pl.BlockSpec((TOP_K, tm), lambda i: (0, i), memory_space=pltpu.SMEM),
            pl.BlockSpec((TOP_K, tm), lambda i: (0, jnp.minimum(i + 1, n - 1)), memory_space=pltpu.SMEM),
            pl.BlockSpec((tm, TOP_K), lambda i: (i, 0)),
            pl.BlockSpec((tm, d), lambda i: (i, 0)),
            pl.BlockSpec((tm * SUBLANES, LANES), lambda i: (i, 0)),
            pl.BlockSpec((1, 6, d), lambda i: (i // per_seq, 0, 0)),
            pl.BlockSpec((d, ds_), const2),
            pl.BlockSpec((d, ds_), const2),
            pl.BlockSpec((ds_, d), const2),
            pl.BlockSpec((1, d), const2),
            pl.BlockSpec((1, d), const2),
            pl.BlockSpec(memory_space=pl.ANY),
        ],
        out_specs=pl.BlockSpec((tm, d), lambda i: (i, 0)),
        out_shape=jax.ShapeDtypeStruct((t, d), F32),
        scratch_shapes=[
            pltpu.VMEM((2, TOP_K, tm * SUBLANES, LANES), U32),
            pltpu.VMEM((TOP_K, tm, LANES), F32),
            pltpu.SemaphoreType.DMA((2, TOP_K)),
        ],
        compiler_params=pltpu.CompilerParams(
            dimension_semantics=("arbitrary",), vmem_limit_bytes=VMEM_LIMIT_BYTES),
        name="final",
    )(dest, dest, gates_t, x1, h2p, ada3, ws1_b, ws3_b, ws2_b, ln2_g.reshape(1, d), ln2_b.reshape(1, d), ys)


def kernel(x, c, w_ada, b_ada, w_in, b_in, ln_v_g, ln_v_b, w_spatial, b_spatial, w_pool, b_pool, pool_scale, w_out, ln1_g, ln1_b, w_router, router_bias, w1, w3, w2, ws1, ws3, ws2, ln2_g, ln2_b):
    bsz, seq, d = x.shape
    t = bsz * seq
    e = w_router.shape[-1]
    assert w_ada.shape[0] == DEPTH
    l = 0

    ada3 = _ada_call(c, w_ada[l], b_ada[l]).reshape(bsz, 6, d)

    cat = _mix_call(x, ada3, w_in[l].astype(BF16), b_in[l], ln_v_g[l], ln_v_b[l], w_spatial[l],
                    b_spatial[l].T, w_pool[l].astype(BF16), b_pool[l], pool_scale[l])

    wr_t = w_router[l].T
    wr_hi = wr_t.astype(BF16)
    wr_lo = (wr_t - wr_hi.astype(F32)).astype(BF16)
    x1, h2p, logits_t = _out_call(cat.reshape(t, d), x.reshape(t, d), ada3, w_out[l].astype(BF16),
                                  ln1_g[l], ln1_b[l], wr_hi, wr_lo, seq)

    idx, gates, rank, cnt = _route_call(logits_t, router_bias[l])

    tb = GMM_TB
    nb = (t * TOP_K) // tb + e
    dest, last, be, nxt, nxt2, par, nblk = _plan_call(cnt, idx, rank, nb, tb)

    xs = _dispatch_call(dest, h2p, _padzero_call(last[:, 0], nb, tb))

    ys = _gmm_call(be[0], nxt[0], nxt2[0], par[0], nblk[0, :1], xs, w1.reshape(w1.shape[1:]),
                   w3.reshape(w3.shape[1:]), w2.reshape(w2.shape[1:]))

    out = _final_call(dest, gates.T, x1, h2p, ada3, ws1[l].astype(BF16), ws3[l].astype(BF16),
                      ws2[l].astype(BF16), ln2_g[l], ln2_b[l], ys, seq)
    return out.reshape(bsz, seq, d)
```

```python
import functools

import jax
import jax.numpy as jnp
from jax import lax
from jax.experimental import pallas as pl
from jax.experimental.pallas import tpu as pltpu

F32 = jnp.float32
BF16 = jnp.bfloat16
U32 = jnp.uint32
I32 = jnp.int32

N_HEADS_A = 8
GMLP_BLOCK = 128
CHUNK = 64
POOL_WINDOWS = (2, 4, 8, 16)
POOL_HISTORY = 16
N_EXPERT_GROUPS = 8
TOPK_GROUPS = 4
TOP_K = 8
ROUTED_SCALE = 2.5
LN_EPS = 1e-5
DEPTH = 1
DEEPNORM_ALPHA = (2.0 * DEPTH) ** 0.25

LANES = 128
SUBLANES = 8
VMEM_LIMIT_BYTES = 56 * 1024 * 1024
ADA_TN = 1024
MIX_TS = 256
OUT_TM = 512
OUT_SUB = 256
ROUTE_TR = 256
PLAN_TR = 512
GMM_TB = 256
DISP_TM = 512
FIN_TM = 256
FIN_ISSUE_CHUNKS = 4
ROW_DMA_GROUP = 16
PADZERO_GROUP = 8
WEIGHT_SLOTS = 3
WEIGHT_DMA_PRIORITY = 1
CAST_ROWS = 256


def _dot(a, b):
    return jnp.dot(a, b, preferred_element_type=F32)


def _split_bf16(a):
    hi = a.astype(BF16)
    lo = (a - hi.astype(F32)).astype(BF16)
    return hi, lo


def _gelu(x):
    return 0.5 * x * (1.0 + lax.erf(x * 0.7071067811865476))


def _silu(x):
    return x * jax.nn.sigmoid(x)


def _pack_bf16_pair(a, b):
    au = pltpu.bitcast(a.astype(BF16).astype(F32), U32)
    bu = pltpu.bitcast(b.astype(BF16).astype(F32), U32)
    return (au & jnp.uint32(0xFFFF0000)) | (bu >> 16)


def _unpack_bf16_pair(w):
    a = pltpu.bitcast(w & jnp.uint32(0xFFFF0000), F32)
    b = pltpu.bitcast(w << 16, F32)
    return a, b


def _tiles_to_rows(ref, n):
    return jnp.concatenate([ref[pl.ds(c, n, stride=SUBLANES), :] for c in range(SUBLANES)], axis=1)


def _rows_to_tiles(ref, value):
    n = value.shape[0]
    for c in range(SUBLANES):
        ref[pl.ds(c, n, stride=SUBLANES), :] = value[:, c * LANES:(c + 1) * LANES]


def _ada_kernel(c_ref, w_ref, b_ref, o_ref):
    s = _silu(c_ref[...])
    sh, sl = _split_bf16(s)
    wh, wl = _split_bf16(w_ref[...])
    o_ref[...] = _dot(sh, wh) + _dot(sl, wh) + _dot(sh, wl) + b_ref[...]


def _ada_call(c, w_ada, b_ada):
    bsz, d = c.shape
    n = w_ada.shape[1]
    return pl.pallas_call(
        _ada_kernel,
        grid=(n // ADA_TN,),
        in_specs=[
            pl.BlockSpec((bsz, d), lambda j: (0, 0)),
            pl.BlockSpec((d, ADA_TN), lambda j: (0, j)),
            pl.BlockSpec((1, ADA_TN), lambda j: (0, j)),
        ],
        out_specs=pl.BlockSpec((bsz, ADA_TN), lambda j: (0, j)),
        out_shape=jax.ShapeDtypeStruct((bsz, n), F32),
        compiler_params=pltpu.CompilerParams(
            dimension_semantics=("arbitrary",), vmem_limit_bytes=VMEM_LIMIT_BYTES),
        name="ada",
    )(c, w_ada, b_ada.reshape(1, n))


def _mix_kernel(x_ref, ada_ref, win_ref, bin_ref, lng_ref, lnb_ref, wsp_ref, bsp_ref,
                wpool_ref, bpool_ref, pscale_ref, o_ref, carry_ref, *, d_a, d_g):
    j = pl.program_id(1)
    ts = x_ref.shape[1]
    x = x_ref[0]
    shift = ada_ref[0, 0:1, :]
    scale = ada_ref[0, 1:2, :]
    h = (x * (1.0 + scale) + shift).astype(BF16)
    proj = _dot(h, win_ref[...]) + bin_ref[...]

    u = _gelu(proj[:, :d_a])
    v = _gelu(proj[:, d_a:2 * d_a])
    mu = jnp.mean(v, axis=-1, keepdims=True)
    vc = v - mu
    var = jnp.mean(vc * vc, axis=-1, keepdims=True)
    vn = (vc * lax.rsqrt(var + LN_EPS) * lng_ref[...] + lnb_ref[...]).astype(BF16)
    row = lax.broadcasted_iota(I32, (GMLP_BLOCK, GMLP_BLOCK), 0)
    col = lax.broadcasted_iota(I32, (GMLP_BLOCK, GMLP_BLOCK), 1)
    causal = (col // CHUNK) <= (row // CHUNK)
    hd = d_a // N_HEADS_A
    for head in range(N_HEADS_A):
        w = jnp.where(causal, wsp_ref[head], 0.0).astype(BF16)
        bias = bsp_ref[:, head:head + 1]
        cs = slice(head * hd, (head + 1) * hd)
        for n in range(ts // GMLP_BLOCK):
            rs = slice(n * GMLP_BLOCK, (n + 1) * GMLP_BLOCK)
            mixed = _dot(w, vn[rs, cs]) + bias
            o_ref[0, rs, cs] = (u[rs, cs] * mixed).astype(BF16)

    z = proj[:, 2 * d_a:]

    @pl.when(j == 0)
    def _():
        carry_ref[...] = jnp.zeros_like(carry_ref)

    ext = jnp.concatenate([carry_ref[...], z], axis=0)
    carry_ref[...] = z[ts - POOL_HISTORY:, :]
    tpos = j * ts + lax.broadcasted_iota(I32, (ts, 1), 0)
    for g, win in enumerate(POOL_WINDOWS):
        gs = slice(g * d_g, (g + 1) * d_g)
        s = ext[:, gs]
        sh = 1
        while sh < win:
            s = s + pltpu.roll(s, sh, axis=0)
            sh *= 2
        cnt = jnp.minimum(tpos + 1, win).astype(F32)
        pooled = s[POOL_HISTORY:, :] / cnt - z[:, gs]
        y = _dot(pooled.astype(BF16), wpool_ref[g]) + bpool_ref[g:g + 1, :]
        o_ref[0, :, d_a + g * d_g:d_a + (g + 1) * d_g] = (y * pscale_ref[:, gs]).astype(BF16)


def _mix_call(x, ada3, w_in_b, b_in, ln_v_g, ln_v_b, w_spatial, b_spatial_t, w_pool_b, b_pool, pool_scale):
    bsz, seq, d = x.shape
    n_proj = w_in_b.shape[1]
    d_a = ln_v_g.shape[0]
    d_b = pool_scale.shape[0]
    n_g, d_g, _ = w_pool_b.shape
    ts = MIX_TS
    const2 = lambda b, j: (0, 0)
    const3 = lambda b, j: (0, 0, 0)
    return pl.pallas_call(
        functools.partial(_mix_kernel, d_a=d_a, d_g=d_g),
        grid=(bsz, seq // ts),
        in_specs=[
            pl.BlockSpec((1, ts, d), lambda b, j: (b, j, 0)),
            pl.BlockSpec((1, 6, d), lambda b, j: (b, 0, 0)),
            pl.BlockSpec((d, n_proj), const2),
            pl.BlockSpec((1, n_proj), const2),
            pl.BlockSpec((1, d_a), const2),
            pl.BlockSpec((1, d_a), const2),
            pl.BlockSpec((N_HEADS_A, GMLP_BLOCK, GMLP_BLOCK), const3),
            pl.BlockSpec((GMLP_BLOCK, N_HEADS_A), const2),
            pl.BlockSpec((n_g, d_g, d_g), const3),
            pl.BlockSpec((n_g, d_g), const2),
            pl.BlockSpec((1, d_b), const2),
        ],
        out_specs=pl.BlockSpec((1, ts, d_a + d_b), lambda b, j: (b, j, 0)),
        out_shape=jax.ShapeDtypeStruct((bsz, seq, d_a + d_b), BF16),
        scratch_shapes=[pltpu.VMEM((POOL_HISTORY, d_b), F32)],
        compiler_params=pltpu.CompilerParams(
            dimension_semantics=("arbitrary", "arbitrary"), vmem_limit_bytes=VMEM_LIMIT_BYTES),
        name="mix",
    )(x, ada3, w_in_b, b_in.reshape(1, n_proj), ln_v_g.reshape(1, d_a), ln_v_b.reshape(1, d_a),
      w_spatial, b_spatial_t, w_pool_b, b_pool, pool_scale.reshape(1, d_b))


def _layer_norm_rows(y, g, b):
    mu = jnp.mean(y, axis=-1, keepdims=True)
    yc = y - mu
    var = jnp.mean(yc * yc, axis=-1, keepdims=True)
    return yc * lax.rsqrt(var + LN_EPS) * g + b


def _out_kernel(cat_ref, x_ref, ada_ref, wout_ref, g_ref, b_ref, wrh_ref, wrl_ref,
                x1_ref, h2p_ref, lgt_ref):
    tm, d = x_ref.shape
    half = d // 2
    gate1 = ada_ref[0, 2:3, :]
    nt = (((1,), (1,)), ((), ()))
    for r0 in range(0, tm, OUT_SUB):
        rows = slice(r0, r0 + OUT_SUB)
        mix = _dot(cat_ref[rows, :], wout_ref[...])
        x1 = _layer_norm_rows(DEEPNORM_ALPHA * x_ref[rows, :] + gate1 * mix, g_ref[...], b_ref[...])
        x1_ref[rows, :] = x1
        h2 = x1 * (1.0 + ada_ref[0, 4:5, :]) + ada_ref[0, 3:4, :]
        _rows_to_tiles(h2p_ref.at[pl.ds(r0 * SUBLANES, OUT_SUB * SUBLANES)],
                       _pack_bf16_pair(h2[:, :half], h2[:, half:]))
        hh, hl = _split_bf16(h2)
        wrh = wrh_ref[...]
        lgt_ref[:, rows] = (lax.dot_general(wrh, hh, nt, preferred_element_type=F32)
                            + lax.dot_general(wrh, hl, nt, preferred_element_type=F32)
                            + lax.dot_general(wrl_ref[...], hh, nt, preferred_element_type=F32))


def _out_call(cat, x2d, ada3, w_out_b, ln1_g, ln1_b, wr_hi_t, wr_lo_t, seq):
    t, d = x2d.shape
    e = wr_hi_t.shape[0]
    tm = OUT_TM
    per_seq = seq // tm
    const2 = lambda i: (0, 0)
    once = pl.Buffered(1)
    return pl.pallas_call(
        _out_kernel,
        grid=(t // tm,),
        in_specs=[
            pl.BlockSpec((tm, d), lambda i: (i, 0)),
            pl.BlockSpec((tm, d), lambda i: (i, 0)),
            pl.BlockSpec((1, 6, d), lambda i: (i // per_seq, 0, 0)),
            pl.BlockSpec((d, d), const2, pipeline_mode=once),
            pl.BlockSpec((1, d), const2),
            pl.BlockSpec((1, d), const2),
            pl.BlockSpec((e, d), const2, pipeline_mode=once),
            pl.BlockSpec((e, d), const2, pipeline_mode=once),
        ],
        out_specs=[
            pl.BlockSpec((tm, d), lambda i: (i, 0)),
            pl.BlockSpec((tm * SUBLANES, LANES), lambda i: (i, 0)),
            pl.BlockSpec((e, tm), lambda i: (0, i)),
        ],
        out_shape=[
            jax.ShapeDtypeStruct((t, d), F32),
            jax.ShapeDtypeStruct((t * SUBLANES, LANES), U32),
            jax.ShapeDtypeStruct((e, t), F32),
        ],
        compiler_params=pltpu.CompilerParams(
            dimension_semantics=("arbitrary",), vmem_limit_bytes=VMEM_LIMIT_BYTES),
        name="out",
    )(cat, x2d, ada3, w_out_b, ln1_g.reshape(1, d), ln1_b.reshape(1, d), wr_hi_t, wr_lo_t)


def _route_kernel(lg_ref, bias_ref, idx_ref, gate_ref, rank_ref, cnt_ref, run_ref, gs_ref, keep_ref):
    i = pl.program_id(0)
    e, tr = lg_ref.shape
    per_group = e // N_EXPERT_GROUPS
    neg = -jnp.inf

    @pl.when(i == 0)
    def _():
        run_ref[...] = jnp.zeros_like(run_ref)

    scores = jax.nn.sigmoid(lg_ref[...])
    sel = scores + bias_ref[...]
    rowi = lax.broadcasted_iota(I32, (e, tr), 0)

    for g in range(N_EXPERT_GROUPS):
        rs = slice(g * per_group, (g + 1) * per_group)
        blk = sel[rs]
        ri = lax.broadcasted_iota(I32, (per_group, tr), 0) + g * per_group
        m1 = jnp.max(blk, axis=0, keepdims=True)
        i1 = jnp.min(jnp.where(blk == m1, ri, e), axis=0, keepdims=True)
        m2 = jnp.max(jnp.where(ri == i1, neg, blk), axis=0, keepdims=True)
        gs_ref[g:g + 1, :] = m1 + m2
    gmat = gs_ref[...]
    gi = lax.broadcasted_iota(I32, gmat.shape, 0)
    beaten = jnp.zeros(gmat.shape, I32)
    for g in range(N_EXPERT_GROUPS):
        gj = gs_ref[g:g + 1, :]
        beats = jnp.where(gj > gmat, 1, jnp.where((gj == gmat) & (gi > g), 1, 0))
        beaten = beaten + beats
    keep_ref[...] = jnp.where(beaten < TOPK_GROUPS, 1.0, 0.0)
    ekeep = jnp.concatenate(
        [jnp.broadcast_to(keep_ref[g:g + 1, :], (per_group, tr)) for g in range(N_EXPERT_GROUPS)], axis=0)
    masked = jnp.where(ekeep > 0.5, sel, neg)

    idxs, tops = [], []
    for k in range(TOP_K):
        m = jnp.max(masked, axis=0, keepdims=True)
        ik = jnp.min(jnp.where(masked == m, rowi, e), axis=0, keepdims=True)
        hit = rowi == ik
        tops.append(jnp.sum(jnp.where(hit, scores, 0.0), axis=0, keepdims=True))
        masked = jnp.where(hit, neg, masked)
        idxs.append(ik)
        idx_ref[k:k + 1, :] = ik
    total = tops[0]
    for k in range(1, TOP_K):
        total = total + tops[k]
    for k in range(TOP_K):
        gate_ref[k:k + 1, :] = tops[k] / total * ROUTED_SCALE

    chosen = jnp.zeros((e, tr), F32)
    for ik in idxs:
        chosen = chosen + jnp.where(rowi == ik, 1.0, 0.0)
    before = lax.broadcasted_iota(I32, (tr, tr), 0) < lax.broadcasted_iota(I32, (tr, tr), 1)
    upper = jnp.where(before, 1.0, 0.0).astype(BF16)
    pos = _dot(chosen.astype(BF16), upper) + run_ref[...]
    for k, ik in enumerate(idxs):
        rank_ref[k:k + 1, :] = jnp.sum(jnp.where(rowi == ik, pos, 0.0), axis=0, keepdims=True).astype(I32)
    run = run_ref[...] + jnp.sum(chosen, axis=1, keepdims=True)
    run_ref[...] = run
    cnt_ref[...] = jnp.broadcast_to(run, cnt_ref.shape).astype(I32)


def _route_call(logits_t, router_bias):
    e, t = logits_t.shape
    tr = ROUTE_TR
    tok = lambda i: (0, i)
    return pl.pallas_call(
        _route_kernel,
        grid=(t // tr,),
        in_specs=[pl.BlockSpec((e, tr), tok), pl.BlockSpec((e, 1), lambda i: (0, 0))],
        out_specs=[
            pl.BlockSpec((TOP_K, tr), tok),
            pl.BlockSpec((TOP_K, tr), tok),
            pl.BlockSpec((TOP_K, tr), tok),
            pl.BlockSpec((e, LANES), lambda i: (0, 0)),
        ],
        out_shape=[
            jax.ShapeDtypeStruct((TOP_K, t), I32),
            jax.ShapeDtypeStruct((TOP_K, t), F32),
            jax.ShapeDtypeStruct((TOP_K, t), I32),
            jax.ShapeDtypeStruct((e, LANES), I32),
        ],
        scratch_shapes=[pltpu.VMEM((e, 1), F32), pltpu.VMEM((N_EXPERT_GROUPS, tr), F32),
                        pltpu.VMEM((N_EXPERT_GROUPS, tr), F32)],
        compiler_params=pltpu.CompilerParams(
            dimension_semantics=("arbitrary",), vmem_limit_bytes=VMEM_LIMIT_BYTES),
        name="route",
    )(logits_t, router_bias.reshape(e, 1))


def _plan_kernel(cnt_ref, idx_ref, rank_ref, dest_ref, last_ref, be_ref, nxt_ref, par_ref, nblk_ref,
                 pst_ref, *, tb):
    e = cnt_ref.shape[0]
    nb = be_ref.shape[1]
    tr = idx_ref.shape[1]

    @pl.when(pl.program_id(0) == 0)
    def _():
        shift = tb.bit_length() - 1
        nblocks = jnp.right_shift(cnt_ref[...] + (tb - 1), shift).astype(F32)
        lower = jnp.where(lax.broadcasted_iota(I32, (e, e), 1) <= lax.broadcasted_iota(I32, (e, e), 0), 1.0, 0.0)
        bend = _dot(lower.astype(BF16), nblocks.astype(BF16))
        nblk = bend[e - 1:e, :]
        pst_ref[...] = ((bend - nblocks) * tb)[:, 0:1]
        last_ref[...] = jnp.where(nblocks > 0, bend - 1.0, -1.0).astype(I32)
        nblk_ref[...] = jnp.broadcast_to(nblk, nblk_ref.shape).astype(I32)
        blk = lax.broadcasted_iota(I32, (e, nb), 1).astype(F32)
        exp = lax.broadcasted_iota(I32, (e, nb), 0).astype(F32)
        be = jnp.minimum(jnp.sum(jnp.where(bend[:, 0:1] <= blk, 1.0, 0.0), axis=0, keepdims=True), e - 1.0)
        later = jnp.where((nblocks[:, 0:1] > 0) & (exp > be), exp, float(e))
        nxt = jnp.min(later, axis=0, keepdims=True)
        chain = [nxt]
        for _ in range(WEIGHT_SLOTS - 1):
            later = jnp.where((nblocks[:, 0:1] > 0) & (exp > chain[-1]), exp, float(e))
            chain.append(jnp.min(later, axis=0, keepdims=True))
        earlier = jnp.sum(jnp.where((nblocks[:, 0:1] > 0) & (exp < be), 1.0, 0.0), axis=0, keepdims=True)
        slot = earlier - WEIGHT_SLOTS * jnp.floor((earlier + 0.5) / WEIGHT_SLOTS)
        be_ref[...] = jnp.broadcast_to(be, be_ref.shape).astype(I32)
        for j, nx in enumerate(chain):
            nxt_ref[j] = jnp.broadcast_to(jnp.where(nx >= e, -1.0, nx), be_ref.shape).astype(I32)
        par_ref[...] = jnp.broadcast_to(slot, par_ref.shape).astype(I32)

    rowi = lax.broadcasted_iota(I32, (e, tr), 0)
    pst = pst_ref[...]
    for k in range(TOP_K):
        start = jnp.sum(jnp.where(rowi == idx_ref[k:k + 1, :], pst, 0.0), axis=0, keepdims=True)
        dest_ref[k:k + 1, :] = start.astype(I32) + rank_ref[k:k + 1, :]


def _plan_call(cnt, idx, rank, nb, tb):
    e = cnt.shape[0]
    t = idx.shape[1]
    tr = PLAN_TR
    tok = lambda i: (0, i)
    const = lambda i: (0, 0)
    return pl.pallas_call(
        functools.partial(_plan_kernel, tb=tb),
        grid=(t // tr,),
        in_specs=[pl.BlockSpec((e, LANES), const), pl.BlockSpec((TOP_K, tr), tok), pl.BlockSpec((TOP_K, tr), tok)],
        out_specs=[
            pl.BlockSpec((TOP_K, tr), tok),
            pl.BlockSpec((e, LANES), const),
            pl.BlockSpec((SUBLANES, nb), const),
            pl.BlockSpec((WEIGHT_SLOTS, SUBLANES, nb), lambda i: (0, 0, 0)),
            pl.BlockSpec((SUBLANES, nb), const),
            pl.BlockSpec((SUBLANES, LANES), const),
        ],
        out_shape=[
            jax.ShapeDtypeStruct((TOP_K, t), I32),
            jax.ShapeDtypeStruct((e, LANES), I32),
            jax.ShapeDtypeStruct((SUBLANES, nb), I32),
            jax.ShapeDtypeStruct((WEIGHT_SLOTS, SUBLANES, nb), I32),
            jax.ShapeDtypeStruct((SUBLANES, nb), I32),
            jax.ShapeDtypeStruct((SUBLANES, LANES), I32),
        ],
        scratch_shapes=[pltpu.VMEM((e, 1), F32)],
        compiler_params=pltpu.CompilerParams(dimension_semantics=("arbitrary",)),
        name="plan",
    )(cnt, idx, rank)


def _padzero_kernel(last_ref, xs_out, zbuf, sem):
    n_groups = last_ref.shape[0] // PADZERO_GROUP
    rows = zbuf.shape[0]
    zbuf[...] = jnp.zeros_like(zbuf)

    def for_group(g, action):
        for j in range(PADZERO_GROUP):
            blk = last_ref[g * PADZERO_GROUP + j]

            @pl.when(blk >= 0)
            def _():
                action(pltpu.make_async_copy(zbuf, xs_out.at[pl.ds(pl.multiple_of(blk * rows, rows), rows)],
                                             sem.at[lax.rem(g, 2)]))

    def body(g, carry):
        for_group(g, lambda cp: cp.start())

        @pl.when(g > 0)
        def _():
            for_group(g - 1, lambda cp: cp.wait())
        return carry

    lax.fori_loop(0, n_groups, body, 0)
    for_group(n_groups - 1, lambda cp: cp.wait())


def _padzero_call(last_blk, nb, tb):
    grid_spec = pltpu.PrefetchScalarGridSpec(
        num_scalar_prefetch=1,
        grid=(1,),
        in_specs=[],
        out_specs=pl.BlockSpec(memory_space=pl.ANY),
        scratch_shapes=[pltpu.VMEM((tb * SUBLANES, LANES), U32), pltpu.SemaphoreType.DMA((2,))],
    )
    assert last_blk.shape[0] % PADZERO_GROUP == 0
    return pl.pallas_call(
        _padzero_kernel,
        grid_spec=grid_spec,
        out_shape=jax.ShapeDtypeStruct((nb * tb * SUBLANES, LANES), U32),
        compiler_params=pltpu.CompilerParams(dimension_semantics=("arbitrary",)),
        name="padzero",
    )(last_blk)


def _tile_rows(r):
    return pl.ds(pl.multiple_of(r * SUBLANES, SUBLANES), SUBLANES)


def _dispatch_kernel(dest_ref, h_ref, xs_in, xs_out, sem):
    del xs_in
    tm = dest_ref.shape[1]

    def group(g, carry):
        base = pl.multiple_of(g * ROW_DMA_GROUP, ROW_DMA_GROUP)
        for j in range(ROW_DMA_GROUP):
            for k in range(TOP_K):
                pltpu.make_async_copy(h_ref.at[_tile_rows(base + j)], xs_out.at[_tile_rows(dest_ref[k, base + j])],
                                      sem).start(priority=(j * TOP_K + k) % 2)
        return carry

    lax.fori_loop(0, tm // ROW_DMA_GROUP, group, 0)
    for k in range(TOP_K):
        pltpu.make_async_copy(h_ref, xs_out.at[pl.ds(0, tm * SUBLANES)], sem).wait()


def _dispatch_call(dest, h2p, xs):
    t = dest.shape[1]
    tm = DISP_TM
    return pl.pallas_call(
        _dispatch_kernel,
        grid=(t // tm,),
        in_specs=[
            pl.BlockSpec((TOP_K, tm), lambda i: (0, i), memory_space=pltpu.SMEM),
            pl.BlockSpec((tm * SUBLANES, LANES), lambda i: (i, 0)),
            pl.BlockSpec(memory_space=pl.ANY),
        ],
        out_specs=pl.BlockSpec(memory_space=pl.ANY),
        out_shape=jax.ShapeDtypeStruct(xs.shape, xs.dtype),
        scratch_shapes=[pltpu.SemaphoreType.DMA],
        input_output_aliases={2: 0},
        compiler_params=pltpu.CompilerParams(
            dimension_semantics=("arbitrary",), vmem_limit_bytes=VMEM_LIMIT_BYTES),
        name="dispatch",
    )(dest, h2p, xs)


def _gmm_kernel(be_ref, nxt_ref, par_ref, nb_ref, x_ref, w1_hbm, w3_hbm, w2_hbm, o_ref,
                st1, st3, st2, sem, wb1, wb3, wb2):
    i = pl.program_id(0)
    half = st1.shape[1] // 2
    tb = x_ref.shape[0] // SUBLANES
    nblk = nb_ref[0]

    def weight_copies(ex, s):
        return (pltpu.make_async_copy(w1_hbm.at[ex], st1.at[s], sem.at[s, 0]),
                pltpu.make_async_copy(w3_hbm.at[ex], st3.at[s], sem.at[s, 1]),
                pltpu.make_async_copy(w2_hbm.at[ex], st2.at[s], sem.at[s, 2]))

    def start_weights(ex, s):
        @pl.when(ex >= 0)
        def _():
            for cp in weight_copies(ex, s):
                cp.start(priority=WEIGHT_DMA_PRIORITY)

    @pl.when(i == 0)
    def _():
        start_weights(be_ref[0], 0)
        for j in range(WEIGHT_SLOTS - 1):
            start_weights(nxt_ref[j, 0], j + 1)

    prev = jnp.maximum(i - 1, 0)
    prev_fresh = jnp.logical_or(i == 1, be_ref[prev] != be_ref[jnp.maximum(i - 2, 0)])

    @pl.when(jnp.logical_and(jnp.logical_and(i >= 1, i - 1 < nblk), prev_fresh))
    def _():
        start_weights(nxt_ref[WEIGHT_SLOTS - 1, prev], par_ref[prev])

    active = i < nblk
    fresh = jnp.logical_or(i == 0, be_ref[i] != be_ref[jnp.maximum(i - 1, 0)])

    @pl.when(jnp.logical_and(active, fresh))
    def _():
        s = par_ref[i]
        for cp in weight_copies(be_ref[i], s):
            cp.wait()

        xa, xb = _unpack_bf16_pair(_tiles_to_rows(x_ref, tb))
        xs = (xa.astype(BF16), xb.astype(BF16))
        n_chunks = st1.shape[1] // CAST_ROWS
        per_half = n_chunks // 2
        rows2 = st2.shape[1] // n_chunks
        h1 = None
        h3 = None
        for c in range(n_chunks):
            rs = slice(c * CAST_ROWS, (c + 1) * CAST_ROWS)
            wb1[rs, :] = st1[s, rs, :].astype(BF16)
            wb3[rs, :] = st3[s, rs, :].astype(BF16)
            rs2 = slice(c * rows2, (c + 1) * rows2)
            wb2[rs2, :] = st2[s, rs2, :].astype(BF16)
            xc = xs[c // per_half][:, (c % per_half) * CAST_ROWS:(c % per_half + 1) * CAST_ROWS]
            p1 = _dot(xc, wb1[rs, :])
            p3 = _dot(xc, wb3[rs, :])
            h1 = p1 if h1 is None else h1 + p1
            h3 = p3 if h3 is None else h3 + p3
        y = _dot((_silu(h1) * h3).astype(BF16), wb2[...])
        _rows_to_tiles(o_ref, _pack_bf16_pair(y[:, :half], y[:, half:]))

    @pl.when(jnp.logical_and(active, jnp.logical_not(fresh)))
    def _():
        xa, xb = _unpack_bf16_pair(_tiles_to_rows(x_ref, tb))
        xa = xa.astype(BF16)
        xb = xb.astype(BF16)
        h1 = _dot(xa, wb1[:half, :]) + _dot(xb, wb1[half:, :])
        h3 = _dot(xa, wb3[:half, :]) + _dot(xb, wb3[half:, :])
        y = _dot((_silu(h1) * h3).astype(BF16), wb2[...])
        _rows_to_tiles(o_ref, _pack_bf16_pair(y[:, :half], y[:, half:]))

    @pl.when(i == nblk)
    def _():
        o_ref[...] = jnp.zeros_like(o_ref)


def _gmm_call(block_e, next_e, slot, nblk, xs, w1, w3, w2):
    e, d, de = w1.shape
    assert d // 2 == SUBLANES * LANES
    tb = GMM_TB
    nb = block_e.shape[0]
    blk = (tb * SUBLANES, LANES)
    grid_spec = pltpu.PrefetchScalarGridSpec(
        num_scalar_prefetch=4,
        grid=(nb,),
        in_specs=[
            pl.BlockSpec(blk, lambda i, be, nx, pa, n: (jnp.minimum(i, n[0] - 1), 0)),
            pl.BlockSpec(memory_space=pl.ANY),
            pl.BlockSpec(memory_space=pl.ANY),
            pl.BlockSpec(memory_space=pl.ANY),
        ],
        out_specs=pl.BlockSpec(blk, lambda i, be, nx, pa, n: (jnp.minimum(i, n[0]), 0)),
        scratch_shapes=[
            pltpu.VMEM((WEIGHT_SLOTS, d, de), F32),
            pltpu.VMEM((WEIGHT_SLOTS, d, de), F32),
            pltpu.VMEM((WEIGHT_SLOTS, de, d), F32),
            pltpu.SemaphoreType.DMA((WEIGHT_SLOTS, 3)),
            pltpu.VMEM((d, de), BF16),
            pltpu.VMEM((d, de), BF16),
            pltpu.VMEM((de, d), BF16),
        ],
    )
    return pl.pallas_call(
        _gmm_kernel,
        grid_spec=grid_spec,
        out_shape=jax.ShapeDtypeStruct((nb * tb * SUBLANES, LANES), U32),
        compiler_params=pltpu.CompilerParams(
            dimension_semantics=("arbitrary",), vmem_limit_bytes=VMEM_LIMIT_BYTES),
        name="gmm",
    )(block_e, next_e, slot, nblk, xs, w1, w3, w2)


def _final_kernel(destc_ref, destn_ref, gt_ref, x1_ref, h2p_ref, ada_ref, ws1_ref, ws3_ref, ws2_ref,
                  g_ref, b_ref, ys_hbm, o_ref, gbuf, gb_ref, sem):
    i = pl.program_id(0)
    last = pl.num_programs(0) - 1
    tm, d = x1_ref.shape
    half = d // 2
    n_lane_tiles = d // LANES
    slot = lax.rem(i, 2)
    other = 1 - slot

    def start_rows(dest_ref, s, t, j):
        for k in range(TOP_K):
            pltpu.make_async_copy(ys_hbm.at[_tile_rows(dest_ref[k, t])], gbuf.at[s, k, _tile_rows(t)],
                                  sem.at[s, k]).start(priority=(j * TOP_K + k) % 2)

    def wait_rows(s):
        for k in range(TOP_K):
            pltpu.make_async_copy(ys_hbm.at[pl.ds(0, tm * SUBLANES)], gbuf.at[s, k], sem.at[s, k]).wait()

    @pl.when(i == 0)
    def _():
        def group(g, carry):
            base = pl.multiple_of(g * ROW_DMA_GROUP, ROW_DMA_GROUP)
            for j in range(ROW_DMA_GROUP):
                start_rows(destc_ref, 0, base + j, j)
            return carry
        lax.fori_loop(0, tm // ROW_DMA_GROUP, group, 0)

    wait_rows(slot)

    xa, xb = _unpack_bf16_pair(_tiles_to_rows(h2p_ref, tm))
    xa = xa.astype(BF16)
    xb = xb.astype(BF16)
    h1 = _dot(xa, ws1_ref[:half, :]) + _dot(xb, ws1_ref[half:, :])
    h3 = _dot(xa, ws3_ref[:half, :]) + _dot(xb, ws3_ref[half:, :])
    shared = _dot((_silu(h1) * h3).astype(BF16), ws2_ref[...])

    for k in range(TOP_K):
        gb_ref[k] = jnp.broadcast_to(gt_ref[:, k:k + 1], (tm, LANES))
    gate2 = ada_ref[0, 5:6, :]

    per_chunk = tm // FIN_ISSUE_CHUNKS
    total = jnp.zeros((tm, LANES), F32)
    for c in range(SUBLANES):
        ra = jnp.zeros((tm, LANES), F32)
        rb = jnp.zeros((tm, LANES), F32)
        for k in range(TOP_K):
            ya, yb = _unpack_bf16_pair(gbuf[slot, k, pl.ds(c, tm, stride=SUBLANES), :])
            ra = ra + ya * gb_ref[k]
            rb = rb + yb * gb_ref[k]
        for r, cs in ((ra, slice(c * LANES, (c + 1) * LANES)), (rb, slice(half + c * LANES, half + (c + 1) * LANES))):
            y = DEEPNORM_ALPHA * x1_ref[:, cs] + gate2[:, cs] * (shared[:, cs] + r)
            o_ref[:, cs] = y
            total = total + y
        if c < FIN_ISSUE_CHUNKS:
            for t in range(c * per_chunk, (c + 1) * per_chunk):
                start_rows(destn_ref, other, t, t)
    mu = jnp.sum(total, axis=-1, keepdims=True) / d
    sq = jnp.zeros((tm, LANES), F32)
    for c in range(n_lane_tiles):
        yc = o_ref[:, c * LANES:(c + 1) * LANES] - mu
        sq = sq + yc * yc
    inv = lax.rsqrt(jnp.sum(sq, axis=-1, keepdims=True) / d + LN_EPS)
    for c in range(n_lane_tiles):
        cs = slice(c * LANES, (c + 1) * LANES)
        o_ref[:, cs] = (o_ref[:, cs] - mu) * inv * g_ref[:, cs] + b_ref[:, cs]

    @pl.when(i == last)
    def _():
        wait_rows(other)


def _final_call(dest, gates_t, x1, h2p, ada3, ws1_b, ws3_b, ws2_b, ln2_g, ln2_b, ys, seq):
    t, d = x1.shape
    ds_ = ws1_b.shape[1]
    tm = FIN_TM
    n = t // tm
    per_seq = seq // tm
    const2 = lambda i: (0, 0)
    return pl.pallas_call(
        _final_kernel,
        grid=(n,),
        in_specs=[
            pl.BlockSpec((TOP_K, tm), lambda i: (0, i), memory_space=pltpu.SMEM),
            pl.BlockSpec((TOP_K, tm), lambda i: (0, jnp.minimum(i + 1, n - 1)), memory_space=pltpu.SMEM),
            pl.BlockSpec((tm, TOP_K), lambda i: (i, 0)),
            pl.BlockSpec((tm, d), lambda i: (i, 0)),
            pl.BlockSpec((tm * SUBLANES, LANES), lambda i: (i, 0)),
            pl.BlockSpec((1, 6, d), lambda i: (i // per_seq, 0, 0)),
            pl.BlockSpec((d, ds_), const2),
            pl.BlockSpec((d, ds_), const2),
            pl.BlockSpec((ds_, d), const2),
            pl.BlockSpec((1, d), const2),
            pl.BlockSpec((1, d), const2),
            pl.BlockSpec(memory_space=pl.ANY),
        ],
        out_specs=pl.BlockSpec((tm, d), lambda i: (i, 0)),
        out_shape=jax.ShapeDtypeStruct((t, d), F32),
        scratch_shapes=[
            pltpu.VMEM((2, TOP_K, tm * SUBLANES, LANES), U32),
            pltpu.VMEM((TOP_K, tm, LANES), F32),
            pltpu.SemaphoreType.DMA((2, TOP_K)),
        ],
        compiler_params=pltpu.CompilerParams(
            dimension_semantics=("arbitrary",), vmem_limit_bytes=VMEM_LIMIT_BYTES),
        name="final",
    )(dest, dest, gates_t, x1, h2p, ada3, ws1_b, ws3_b, ws2_b, ln2_g.reshape(1, d), ln2_b.reshape(1, d), ys)


def kernel(x, c, w_ada, b_ada, w_in, b_in, ln_v_g, ln_v_b, w_spatial, b_spatial, w_pool, b_pool, pool_scale, w_out, ln1_g, ln1_b, w_router, router_bias, w1, w3, w2, ws1, ws3, ws2, ln2_g, ln2_b):
    bsz, seq, d = x.shape
    t = bsz * seq
    e = w_router.shape[-1]
    assert w_ada.shape[0] == DEPTH
    l = 0

    ada3 = _ada_call(c, w_ada[l], b_ada[l]).reshape(bsz, 6, d)

    cat = _mix_call(x, ada3, w_in[l].astype(BF16), b_in[l], ln_v_g[l], ln_v_b[l], w_spatial[l],
                    b_spatial[l].T, w_pool[l].astype(BF16), b_pool[l], pool_scale[l])

    wr_t = w_router[l].T
    wr_hi = wr_t.astype(BF16)
    wr_lo = (wr_t - wr_hi.astype(F32)).astype(BF16)
    x1, h2p, logits_t = _out_call(cat.reshape(t, d), x.reshape(t, d), ada3, w_out[l].astype(BF16),
                                  ln1_g[l], ln1_b[l], wr_hi, wr_lo, seq)

    idx, gates, rank, cnt = _route_call(logits_t, router_bias[l])

    tb = GMM_TB
    nb = (t * TOP_K) // tb + e
    dest, last, be, nxt, par, nblk = _plan_call(cnt, idx, rank, nb, tb)

    xs = _dispatch_call(dest, h2p, _padzero_call(last[:, 0], nb, tb))

    ys = _gmm_call(be[0], nxt[:, 0, :], par[0], nblk[0, :1], xs, w1.reshape(w1.shape[1:]),
                   w3.reshape(w3.shape[1:]), w2.reshape(w2.shape[1:]))

    out = _final_call(dest, gates.T, x1, h2p, ada3, ws1[l].astype(BF16), ws3[l].astype(BF16),
                      ws2[l].astype(BF16), ln2_g[l], ln2_b[l], ys, seq)
    return out.reshape(bsz, seq, d)
```

```python
import functools

import jax
import jax.numpy as jnp
from jax import lax
from jax.experimental import pallas as pl
from jax.experimental.pallas import tpu as pltpu

F32 = jnp.float32
BF16 = jnp.bfloat16
U32 = jnp.uint32
I32 = jnp.int32

N_HEADS_A = 8
GMLP_BLOCK = 128
CHUNK = 64
POOL_WINDOWS = (2, 4, 8, 16)
POOL_HISTORY = 16
N_EXPERT_GROUPS = 8
TOPK_GROUPS = 4
TOP_K = 8
ROUTED_SCALE = 2.5
LN_EPS = 1e-5
DEPTH = 1
DEEPNORM_ALPHA = (2.0 * DEPTH) ** 0.25

LANES = 128
SUBLANES = 8
VMEM_LIMIT_BYTES = 56 * 1024 * 1024
ADA_TN = 1024
MIX_TS = 512
MIX_SUB = 256
OUT_TM = 512
OUT_SUB = 256
ROUTE_TR = 256
PLAN_TR = 512
GMM_TB = 256
DISP_TM = 512
FIN_TM = 256
FIN_ISSUE_CHUNKS = 4
ROW_DMA_GROUP = 16
PADZERO_GROUP = 8
WEIGHT_SLOTS = 3
WEIGHT_DMA_PRIORITY = 1
CAST_ROWS = 256


def _dot(a, b):
    return jnp.dot(a, b, preferred_element_type=F32)


def _split_bf16(a):
    hi = a.astype(BF16)
    lo = (a - hi.astype(F32)).astype(BF16)
    return hi, lo


def _gelu(x):
    return 0.5 * x * (1.0 + lax.erf(x * 0.7071067811865476))


def _silu(x):
    return x * jax.nn.sigmoid(x)


def _pack_bf16_pair(a, b):
    au = pltpu.bitcast(a.astype(BF16).astype(F32), U32)
    bu = pltpu.bitcast(b.astype(BF16).astype(F32), U32)
    return (au & jnp.uint32(0xFFFF0000)) | (bu >> 16)


def _unpack_bf16_pair(w):
    a = pltpu.bitcast(w & jnp.uint32(0xFFFF0000), F32)
    b = pltpu.bitcast(w << 16, F32)
    return a, b


def _tiles_to_rows(ref, n):
    return jnp.concatenate([ref[pl.ds(c, n, stride=SUBLANES), :] for c in range(SUBLANES)], axis=1)


def _rows_to_tiles(ref, value):
    n = value.shape[0]
    for c in range(SUBLANES):
        ref[pl.ds(c, n, stride=SUBLANES), :] = value[:, c * LANES:(c + 1) * LANES]


def _ada_kernel(c_ref, w_ref, b_ref, o_ref):
    s = _silu(c_ref[...])
    sh, sl = _split_bf16(s)
    wh, wl = _split_bf16(w_ref[...])
    o_ref[...] = _dot(sh, wh) + _dot(sl, wh) + _dot(sh, wl) + b_ref[...]


def _ada_call(c, w_ada, b_ada):
    bsz, d = c.shape
    n = w_ada.shape[1]
    return pl.pallas_call(
        _ada_kernel,
        grid=(n // ADA_TN,),
        in_specs=[
            pl.BlockSpec((bsz, d), lambda j: (0, 0)),
            pl.BlockSpec((d, ADA_TN), lambda j: (0, j)),
            pl.BlockSpec((1, ADA_TN), lambda j: (0, j)),
        ],
        out_specs=pl.BlockSpec((bsz, ADA_TN), lambda j: (0, j)),
        out_shape=jax.ShapeDtypeStruct((bsz, n), F32),
        compiler_params=pltpu.CompilerParams(
            dimension_semantics=("arbitrary",), vmem_limit_bytes=VMEM_LIMIT_BYTES),
        name="ada",
    )(c, w_ada, b_ada.reshape(1, n))


def _mix_kernel(x_ref, ada_ref, win_ref, bin_ref, lng_ref, lnb_ref, wsp_ref, bsp_ref,
                wpool_ref, bpool_ref, pscale_ref, o_ref, carry_ref, *, d_a, d_g):
    j = pl.program_id(1)
    shift = ada_ref[0, 0:1, :]
    scale = ada_ref[0, 1:2, :]
    row = lax.broadcasted_iota(I32, (GMLP_BLOCK, GMLP_BLOCK), 0)
    col = lax.broadcasted_iota(I32, (GMLP_BLOCK, GMLP_BLOCK), 1)
    causal = (col // CHUNK) <= (row // CHUNK)
    hd = d_a // N_HEADS_A

    @pl.when(j == 0)
    def _():
        carry_ref[...] = jnp.zeros_like(carry_ref)

    ts = MIX_SUB
    for r0 in range(0, x_ref.shape[1], ts):
        h = (x_ref[0, r0:r0 + ts, :] * (1.0 + scale) + shift).astype(BF16)
        proj = _dot(h, win_ref[...]) + bin_ref[...]

        u = _gelu(proj[:, :d_a])
        v = _gelu(proj[:, d_a:2 * d_a])
        mu = jnp.mean(v, axis=-1, keepdims=True)
        vc = v - mu
        var = jnp.mean(vc * vc, axis=-1, keepdims=True)
        vn = (vc * lax.rsqrt(var + LN_EPS) * lng_ref[...] + lnb_ref[...]).astype(BF16)
        for head in range(N_HEADS_A):
            w = jnp.where(causal, wsp_ref[head], 0.0).astype(BF16)
            bias = bsp_ref[:, head:head + 1]
            cs = slice(head * hd, (head + 1) * hd)
            for n in range(ts // GMLP_BLOCK):
                rs = slice(n * GMLP_BLOCK, (n + 1) * GMLP_BLOCK)
                mixed = _dot(w, vn[rs, cs]) + bias
                o_ref[0, r0 + n * GMLP_BLOCK:r0 + (n + 1) * GMLP_BLOCK, cs] = (u[rs, cs] * mixed).astype(BF16)

        z = proj[:, 2 * d_a:]
        ext = jnp.concatenate([carry_ref[...], z], axis=0)
        carry_ref[...] = z[ts - POOL_HISTORY:, :]
        tpos = j * x_ref.shape[1] + r0 + lax.broadcasted_iota(I32, (ts, 1), 0)
        for g, win in enumerate(POOL_WINDOWS):
            gs = slice(g * d_g, (g + 1) * d_g)
            s = ext[:, gs]
            sh = 1
            while sh < win:
                s = s + pltpu.roll(s, sh, axis=0)
                sh *= 2
            cnt = jnp.minimum(tpos + 1, win).astype(F32)
            pooled = s[POOL_HISTORY:, :] / cnt - z[:, gs]
            y = _dot(pooled.astype(BF16), wpool_ref[g]) + bpool_ref[g:g + 1, :]
            o_ref[0, r0:r0 + ts, d_a + g * d_g:d_a + (g + 1) * d_g] = (y * pscale_ref[:, gs]).astype(BF16)


def _mix_call(x, ada3, w_in_b, b_in, ln_v_g, ln_v_b, w_spatial, b_spatial_t, w_pool_b, b_pool, pool_scale):
    bsz, seq, d = x.shape
    n_proj = w_in_b.shape[1]
    d_a = ln_v_g.shape[0]
    d_b = pool_scale.shape[0]
    n_g, d_g, _ = w_pool_b.shape
    ts = MIX_TS
    const2 = lambda b, j: (0, 0)
    const3 = lambda b, j: (0, 0, 0)
    return pl.pallas_call(
        functools.partial(_mix_kernel, d_a=d_a, d_g=d_g),
        grid=(bsz, seq // ts),
        in_specs=[
            pl.BlockSpec((1, ts, d), lambda b, j: (b, j, 0)),
            pl.BlockSpec((1, 6, d), lambda b, j: (b, 0, 0)),
            pl.BlockSpec((d, n_proj), const2, pipeline_mode=pl.Buffered(1)),
            pl.BlockSpec((1, n_proj), const2),
            pl.BlockSpec((1, d_a), const2),
            pl.BlockSpec((1, d_a), const2),
            pl.BlockSpec((N_HEADS_A, GMLP_BLOCK, GMLP_BLOCK), const3),
            pl.BlockSpec((GMLP_BLOCK, N_HEADS_A), const2),
            pl.BlockSpec((n_g, d_g, d_g), const3),
            pl.BlockSpec((n_g, d_g), const2),
            pl.BlockSpec((1, d_b), const2),
        ],
        out_specs=pl.BlockSpec((1, ts, d_a + d_b), lambda b, j: (b, j, 0)),
        out_shape=jax.ShapeDtypeStruct((bsz, seq, d_a + d_b), BF16),
        scratch_shapes=[pltpu.VMEM((POOL_HISTORY, d_b), F32)],
        compiler_params=pltpu.CompilerParams(
            dimension_semantics=("arbitrary", "arbitrary"), vmem_limit_bytes=VMEM_LIMIT_BYTES),
        name="mix",
    )(x, ada3, w_in_b, b_in.reshape(1, n_proj), ln_v_g.reshape(1, d_a), ln_v_b.reshape(1, d_a),
      w_spatial, b_spatial_t, w_pool_b, b_pool, pool_scale.reshape(1, d_b))


def _layer_norm_rows(y, g, b):
    mu = jnp.mean(y, axis=-1, keepdims=True)
    yc = y - mu
    var = jnp.mean(yc * yc, axis=-1, keepdims=True)
    return yc * lax.rsqrt(var + LN_EPS) * g + b


def _out_kernel(cat_ref, x_ref, ada_ref, wout_ref, g_ref, b_ref, wrh_ref, wrl_ref,
                x1_ref, h2p_ref, lgt_ref):
    tm, d = x_ref.shape
    half = d // 2
    gate1 = ada_ref[0, 2:3, :]
    nt = (((1,), (1,)), ((), ()))
    for r0 in range(0, tm, OUT_SUB):
        rows = slice(r0, r0 + OUT_SUB)
        mix = _dot(cat_ref[rows, :], wout_ref[...])
        x1 = _layer_norm_rows(DEEPNORM_ALPHA * x_ref[rows, :] + gate1 * mix, g_ref[...], b_ref[...])
        x1_ref[rows, :] = x1
        h2 = x1 * (1.0 + ada_ref[0, 4:5, :]) + ada_ref[0, 3:4, :]
        _rows_to_tiles(h2p_ref.at[pl.ds(r0 * SUBLANES, OUT_SUB * SUBLANES)],
                       _pack_bf16_pair(h2[:, :half], h2[:, half:]))
        hh, hl = _split_bf16(h2)
        wrh = wrh_ref[...]
        lgt_ref[:, rows] = (lax.dot_general(wrh, hh, nt, preferred_element_type=F32)
                            + lax.dot_general(wrh, hl, nt, preferred_element_type=F32)
                            + lax.dot_general(wrl_ref[...], hh, nt, preferred_element_type=F32))


def _out_call(cat, x2d, ada3, w_out_b, ln1_g, ln1_b, wr_hi_t, wr_lo_t, seq):
    t, d = x2d.shape
    e = wr_hi_t.shape[0]
    tm = OUT_TM
    per_seq = seq // tm
    const2 = lambda i: (0, 0)
    once = pl.Buffered(1)
    return pl.pallas_call(
        _out_kernel,
        grid=(t // tm,),
        in_specs=[
            pl.BlockSpec((tm, d), lambda i: (i, 0)),
            pl.BlockSpec((tm, d), lambda i: (i, 0)),
            pl.BlockSpec((1, 6, d), lambda i: (i // per_seq, 0, 0)),
            pl.BlockSpec((d, d), const2, pipeline_mode=once),
            pl.BlockSpec((1, d), const2),
            pl.BlockSpec((1, d), const2),
            pl.BlockSpec((e, d), const2, pipeline_mode=once),
            pl.BlockSpec((e, d), const2, pipeline_mode=once),
        ],
        out_specs=[
            pl.BlockSpec((tm, d), lambda i: (i, 0)),
            pl.BlockSpec((tm * SUBLANES, LANES), lambda i: (i, 0)),
            pl.BlockSpec((e, tm), lambda i: (0, i)),
        ],
        out_shape=[
            jax.ShapeDtypeStruct((t, d), F32),
            jax.ShapeDtypeStruct((t * SUBLANES, LANES), U32),
            jax.ShapeDtypeStruct((e, t), F32),
        ],
        compiler_params=pltpu.CompilerParams(
            dimension_semantics=("arbitrary",), vmem_limit_bytes=VMEM_LIMIT_BYTES),
        name="out",
    )(cat, x2d, ada3, w_out_b, ln1_g.reshape(1, d), ln1_b.reshape(1, d), wr_hi_t, wr_lo_t)


def _route_kernel(lg_ref, bias_ref, idx_ref, gate_ref, rank_ref, cnt_ref, run_ref, gs_ref, keep_ref):
    i = pl.program_id(0)
    e, tr = lg_ref.shape
    per_group = e // N_EXPERT_GROUPS
    neg = -jnp.inf

    @pl.when(i == 0)
    def _():
        run_ref[...] = jnp.zeros_like(run_ref)

    scores = jax.nn.sigmoid(lg_ref[...])
    sel = scores + bias_ref[...]
    rowi = lax.broadcasted_iota(I32, (e, tr), 0)

    for g in range(N_EXPERT_GROUPS):
        rs = slice(g * per_group, (g + 1) * per_group)
        blk = sel[rs]
        ri = lax.broadcasted_iota(I32, (per_group, tr), 0) + g * per_group
        m1 = jnp.max(blk, axis=0, keepdims=True)
        i1 = jnp.min(jnp.where(blk == m1, ri, e), axis=0, keepdims=True)
        m2 = jnp.max(jnp.where(ri == i1, neg, blk), axis=0, keepdims=True)
        gs_ref[g:g + 1, :] = m1 + m2
    gmat = gs_ref[...]
    gi = lax.broadcasted_iota(I32, gmat.shape, 0)
    beaten = jnp.zeros(gmat.shape, I32)
    for g in range(N_EXPERT_GROUPS):
        gj = gs_ref[g:g + 1, :]
        beats = jnp.where(gj > gmat, 1, jnp.where((gj == gmat) & (gi > g), 1, 0))
        beaten = beaten + beats
    keep_ref[...] = jnp.where(beaten < TOPK_GROUPS, 1.0, 0.0)
    ekeep = jnp.concatenate(
        [jnp.broadcast_to(keep_ref[g:g + 1, :], (per_group, tr)) for g in range(N_EXPERT_GROUPS)], axis=0)
    masked = jnp.where(ekeep > 0.5, sel, neg)

    idxs, tops = [], []
    for k in range(TOP_K):
        m = jnp.max(masked, axis=0, keepdims=True)
        ik = jnp.min(jnp.where(masked == m, rowi, e), axis=0, keepdims=True)
        hit = rowi == ik
        tops.append(jnp.sum(jnp.where(hit, scores, 0.0), axis=0, keepdims=True))
        masked = jnp.where(hit, neg, masked)
        idxs.append(ik)
        idx_ref[k:k + 1, :] = ik
    total = tops[0]
    for k in range(1, TOP_K):
        total = total + tops[k]
    for k in range(TOP_K):
        gate_ref[k:k + 1, :] = tops[k] / total * ROUTED_SCALE

    chosen = jnp.zeros((e, tr), F32)
    for ik in idxs:
        chosen = chosen + jnp.where(rowi == ik, 1.0, 0.0)
    before = lax.broadcasted_iota(I32, (tr, tr), 0) < lax.broadcasted_iota(I32, (tr, tr), 1)
    upper = jnp.where(before, 1.0, 0.0).astype(BF16)
    pos = _dot(chosen.astype(BF16), upper) + run_ref[...]
    for k, ik in enumerate(idxs):
        rank_ref[k:k + 1, :] = jnp.sum(jnp.where(rowi == ik, pos, 0.0), axis=0, keepdims=True).astype(I32)
    run = run_ref[...] + jnp.sum(chosen, axis=1, keepdims=True)
    run_ref[...] = run
    cnt_ref[...] = jnp.broadcast_to(run, cnt_ref.shape).astype(I32)


def _route_call(logits_t, router_bias):
    e, t = logits_t.shape
    tr = ROUTE_TR
    tok = lambda i: (0, i)
    return pl.pallas_call(
        _route_kernel,
        grid=(t // tr,),
        in_specs=[pl.BlockSpec((e, tr), tok), pl.BlockSpec((e, 1), lambda i: (0, 0))],
        out_specs=[
            pl.BlockSpec((TOP_K, tr), tok),
            pl.BlockSpec((TOP_K, tr), tok),
            pl.BlockSpec((TOP_K, tr), tok),
            pl.BlockSpec((e, LANES), lambda i: (0, 0)),
        ],
        out_shape=[
            jax.ShapeDtypeStruct((TOP_K, t), I32),
            jax.ShapeDtypeStruct((TOP_K, t), F32),
            jax.ShapeDtypeStruct((TOP_K, t), I32),
            jax.ShapeDtypeStruct((e, LANES), I32),
        ],
        scratch_shapes=[pltpu.VMEM((e, 1), F32), pltpu.VMEM((N_EXPERT_GROUPS, tr), F32),
                        pltpu.VMEM((N_EXPERT_GROUPS, tr), F32)],
        compiler_params=pltpu.CompilerParams(
            dimension_semantics=("arbitrary",), vmem_limit_bytes=VMEM_LIMIT_BYTES),
        name="route",
    )(logits_t, router_bias.reshape(e, 1))


def _plan_kernel(cnt_ref, idx_ref, rank_ref, dest_ref, last_ref, be_ref, nxt_ref, par_ref, nblk_ref,
                 pst_ref, *, tb):
    e = cnt_ref.shape[0]
    nb = be_ref.shape[1]
    tr = idx_ref.shape[1]

    @pl.when(pl.program_id(0) == 0)
    def _():
        shift = tb.bit_length() - 1
        nblocks = jnp.right_shift(cnt_ref[...] + (tb - 1), shift).astype(F32)
        lower = jnp.where(lax.broadcasted_iota(I32, (e, e), 1) <= lax.broadcasted_iota(I32, (e, e), 0), 1.0, 0.0)
        bend = _dot(lower.astype(BF16), nblocks.astype(BF16))
        nblk = bend[e - 1:e, :]
        pst_ref[...] = ((bend - nblocks) * tb)[:, 0:1]
        last_ref[...] = jnp.where(nblocks > 0, bend - 1.0, -1.0).astype(I32)
        nblk_ref[...] = jnp.broadcast_to(nblk, nblk_ref.shape).astype(I32)
        blk = lax.broadcasted_iota(I32, (e, nb), 1).astype(F32)
        exp = lax.broadcasted_iota(I32, (e, nb), 0).astype(F32)
        be = jnp.minimum(jnp.sum(jnp.where(bend[:, 0:1] <= blk, 1.0, 0.0), axis=0, keepdims=True), e - 1.0)
        later = jnp.where((nblocks[:, 0:1] > 0) & (exp > be), exp, float(e))
        nxt = jnp.min(later, axis=0, keepdims=True)
        chain = [nxt]
        for _ in range(WEIGHT_SLOTS - 1):
            later = jnp.where((nblocks[:, 0:1] > 0) & (exp > chain[-1]), exp, float(e))
            chain.append(jnp.min(later, axis=0, keepdims=True))
        earlier = jnp.sum(jnp.where((nblocks[:, 0:1] > 0) & (exp < be), 1.0, 0.0), axis=0, keepdims=True)
        slot = earlier - WEIGHT_SLOTS * jnp.floor((earlier + 0.5) / WEIGHT_SLOTS)
        be_ref[...] = jnp.broadcast_to(be, be_ref.shape).astype(I32)
        for j, nx in enumerate(chain):
            nxt_ref[j] = jnp.broadcast_to(jnp.where(nx >= e, -1.0, nx), be_ref.shape).astype(I32)
        par_ref[...] = jnp.broadcast_to(slot, par_ref.shape).astype(I32)

    rowi = lax.broadcasted_iota(I32, (e, tr), 0)
    pst = pst_ref[...]
    for k in range(TOP_K):
        start = jnp.sum(jnp.where(rowi == idx_ref[k:k + 1, :], pst, 0.0), axis=0, keepdims=True)
        dest_ref[k:k + 1, :] = start.astype(I32) + rank_ref[k:k + 1, :]


def _plan_call(cnt, idx, rank, nb, tb):
    e = cnt.shape[0]
    t = idx.shape[1]
    tr = PLAN_TR
    tok = lambda i: (0, i)
    const = lambda i: (0, 0)
    return pl.pallas_call(
        functools.partial(_plan_kernel, tb=tb),
        grid=(t // tr,),
        in_specs=[pl.BlockSpec((e, LANES), const), pl.BlockSpec((TOP_K, tr), tok), pl.BlockSpec((TOP_K, tr), tok)],
        out_specs=[
            pl.BlockSpec((TOP_K, tr), tok),
            pl.BlockSpec((e, LANES), const),
            pl.BlockSpec((SUBLANES, nb), const),
            pl.BlockSpec((WEIGHT_SLOTS, SUBLANES, nb), lambda i: (0, 0, 0)),
            pl.BlockSpec((SUBLANES, nb), const),
            pl.BlockSpec((SUBLANES, LANES), const),
        ],
        out_shape=[
            jax.ShapeDtypeStruct((TOP_K, t), I32),
            jax.ShapeDtypeStruct((e, LANES), I32),
            jax.ShapeDtypeStruct((SUBLANES, nb), I32),
            jax.ShapeDtypeStruct((WEIGHT_SLOTS, SUBLANES, nb), I32),
            jax.ShapeDtypeStruct((SUBLANES, nb), I32),
            jax.ShapeDtypeStruct((SUBLANES, LANES), I32),
        ],
        scratch_shapes=[pltpu.VMEM((e, 1), F32)],
        compiler_params=pltpu.CompilerParams(dimension_semantics=("arbitrary",)),
        name="plan",
    )(cnt, idx, rank)


def _padzero_kernel(last_ref, xs_out, zbuf, sem):
    n_groups = last_ref.shape[0] // PADZERO_GROUP
    rows = zbuf.shape[0]
    zbuf[...] = jnp.zeros_like(zbuf)

    def for_group(g, action):
        for j in range(PADZERO_GROUP):
            blk = last_ref[g * PADZERO_GROUP + j]

            @pl.when(blk >= 0)
            def _():
                action(pltpu.make_async_copy(zbuf, xs_out.at[pl.ds(pl.multiple_of(blk * rows, rows), rows)],
                                             sem.at[lax.rem(g, 2)]))

    def body(g, carry):
        for_group(g, lambda cp: cp.start())

        @pl.when(g > 0)
        def _():
            for_group(g - 1, lambda cp: cp.wait())
        return carry

    lax.fori_loop(0, n_groups, body, 0)
    for_group(n_groups - 1, lambda cp: cp.wait())


def _padzero_call(last_blk, nb, tb):
    grid_spec = pltpu.PrefetchScalarGridSpec(
        num_scalar_prefetch=1,
        grid=(1,),
        in_specs=[],
        out_specs=pl.BlockSpec(memory_space=pl.ANY),
        scratch_shapes=[pltpu.VMEM((tb * SUBLANES, LANES), U32), pltpu.SemaphoreType.DMA((2,))],
    )
    assert last_blk.shape[0] % PADZERO_GROUP == 0
    return pl.pallas_call(
        _padzero_kernel,
        grid_spec=grid_spec,
        out_shape=jax.ShapeDtypeStruct((nb * tb * SUBLANES, LANES), U32),
        compiler_params=pltpu.CompilerParams(dimension_semantics=("arbitrary",)),
        name="padzero",
    )(last_blk)


def _tile_rows(r):
    return pl.ds(pl.multiple_of(r * SUBLANES, SUBLANES), SUBLANES)


def _dispatch_kernel(dest_ref, h_ref, xs_in, xs_out, sem):
    del xs_in
    tm = dest_ref.shape[1]

    def group(g, carry):
        base = pl.multiple_of(g * ROW_DMA_GROUP, ROW_DMA_GROUP)
        for j in range(ROW_DMA_GROUP):
            for k in range(TOP_K):
                pltpu.make_async_copy(h_ref.at[_tile_rows(base + j)], xs_out.at[_tile_rows(dest_ref[k, base + j])],
                                      sem).start(priority=(j * TOP_K + k) % 2)
        return carry

    lax.fori_loop(0, tm // ROW_DMA_GROUP, group, 0)
    for k in range(TOP_K):
        pltpu.make_async_copy(h_ref, xs_out.at[pl.ds(0, tm * SUBLANES)], sem).wait()


def _dispatch_call(dest, h2p, xs):
    t = dest.shape[1]
    tm = DISP_TM
    return pl.pallas_call(
        _dispatch_kernel,
        grid=(t // tm,),
        in_specs=[
            pl.BlockSpec((TOP_K, tm), lambda i: (0, i), memory_space=pltpu.SMEM),
            pl.BlockSpec((tm * SUBLANES, LANES), lambda i: (i, 0)),
            pl.BlockSpec(memory_space=pl.ANY),
        ],
        out_specs=pl.BlockSpec(memory_space=pl.ANY),
        out_shape=jax.ShapeDtypeStruct(xs.shape, xs.dtype),
        scratch_shapes=[pltpu.SemaphoreType.DMA],
        input_output_aliases={2: 0},
        compiler_params=pltpu.CompilerParams(
            dimension_semantics=("arbitrary",), vmem_limit_bytes=VMEM_LIMIT_BYTES),
        name="dispatch",
    )(dest, h2p, xs)


def _gmm_kernel(be_ref, nxt_ref, par_ref, nb_ref, x_ref, w1_hbm, w3_hbm, w2_hbm, o_ref,
                st1, st3, st2, sem, wb1, wb3, wb2):
    i = pl.program_id(0)
    half = st1.shape[1] // 2
    tb = x_ref.shape[0] // SUBLANES
    nblk = nb_ref[0]

    def weight_copies(ex, s):
        return (pltpu.make_async_copy(w1_hbm.at[ex], st1.at[s], sem.at[s, 0]),
                pltpu.make_async_copy(w3_hbm.at[ex], st3.at[s], sem.at[s, 1]),
                pltpu.make_async_copy(w2_hbm.at[ex], st2.at[s], sem.at[s, 2]))

    def start_weights(ex, s):
        @pl.when(ex >= 0)
        def _():
            for cp in weight_copies(ex, s):
                cp.start(priority=WEIGHT_DMA_PRIORITY)

    @pl.when(i == 0)
    def _():
        start_weights(be_ref[0], 0)
        for j in range(WEIGHT_SLOTS - 1):
            start_weights(nxt_ref[j, 0], j + 1)

    prev = jnp.maximum(i - 1, 0)
    prev_fresh = jnp.logical_or(i == 1, be_ref[prev] != be_ref[jnp.maximum(i - 2, 0)])

    @pl.when(jnp.logical_and(jnp.logical_and(i >= 1, i - 1 < nblk), prev_fresh))
    def _():
        start_weights(nxt_ref[WEIGHT_SLOTS - 1, prev], par_ref[prev])

    active = i < nblk
    fresh = jnp.logical_or(i == 0, be_ref[i] != be_ref[jnp.maximum(i - 1, 0)])

    @pl.when(jnp.logical_and(active, fresh))
    def _():
        s = par_ref[i]
        for cp in weight_copies(be_ref[i], s):
            cp.wait()

        xa, xb = _unpack_bf16_pair(_tiles_to_rows(x_ref, tb))
        xs = (xa.astype(BF16), xb.astype(BF16))
        n_chunks = st1.shape[1] // CAST_ROWS
        per_half = n_chunks // 2
        rows2 = st2.shape[1] // n_chunks
        h1 = None
        h3 = None
        for c in range(n_chunks):
            rs = slice(c * CAST_ROWS, (c + 1) * CAST_ROWS)
            wb1[rs, :] = st1[s, rs, :].astype(BF16)
            wb3[rs, :] = st3[s, rs, :].astype(BF16)
            rs2 = slice(c * rows2, (c + 1) * rows2)
            wb2[rs2, :] = st2[s, rs2, :].astype(BF16)
            xc = xs[c // per_half][:, (c % per_half) * CAST_ROWS:(c % per_half + 1) * CAST_ROWS]
            p1 = _dot(xc, wb1[rs, :])
            p3 = _dot(xc, wb3[rs, :])
            h1 = p1 if h1 is None else h1 + p1
            h3 = p3 if h3 is None else h3 + p3
        y = _dot((_silu(h1) * h3).astype(BF16), wb2[...])
        _rows_to_tiles(o_ref, _pack_bf16_pair(y[:, :half], y[:, half:]))

    @pl.when(jnp.logical_and(active, jnp.logical_not(fresh)))
    def _():
        xa, xb = _unpack_bf16_pair(_tiles_to_rows(x_ref, tb))
        xa = xa.astype(BF16)
        xb = xb.astype(BF16)
        h1 = _dot(xa, wb1[:half, :]) + _dot(xb, wb1[half:, :])
        h3 = _dot(xa, wb3[:half, :]) + _dot(xb, wb3[half:, :])
        y = _dot((_silu(h1) * h3).astype(BF16), wb2[...])
        _rows_to_tiles(o_ref, _pack_bf16_pair(y[:, :half], y[:, half:]))

    @pl.when(i == nblk)
    def _():
        o_ref[...] = jnp.zeros_like(o_ref)


def _gmm_call(block_e, next_e, slot, nblk, xs, w1, w3, w2):
    e, d, de = w1.shape
    assert d // 2 == SUBLANES * LANES
    tb = GMM_TB
    nb = block_e.shape[0]
    blk = (tb * SUBLANES, LANES)
    grid_spec = pltpu.PrefetchScalarGridSpec(
        num_scalar_prefetch=4,
        grid=(nb,),
        in_specs=[
            pl.BlockSpec(blk, lambda i, be, nx, pa, n: (jnp.minimum(i, n[0] - 1), 0)),
            pl.BlockSpec(memory_space=pl.ANY),
            pl.BlockSpec(memory_space=pl.ANY),
            pl.BlockSpec(memory_space=pl.ANY),
        ],
        out_specs=pl.BlockSpec(blk, lambda i, be, nx, pa, n: (jnp.minimum(i, n[0]), 0)),
        scratch_shapes=[
            pltpu.VMEM((WEIGHT_SLOTS, d, de), F32),
            pltpu.VMEM((WEIGHT_SLOTS, d, de), F32),
            pltpu.VMEM((WEIGHT_SLOTS, de, d), F32),
            pltpu.SemaphoreType.DMA((WEIGHT_SLOTS, 3)),
            pltpu.VMEM((d, de), BF16),
            pltpu.VMEM((d, de), BF16),
            pltpu.VMEM((de, d), BF16),
        ],
    )
    return pl.pallas_call(
        _gmm_kernel,
        grid_spec=grid_spec,
        out_shape=jax.ShapeDtypeStruct((nb * tb * SUBLANES, LANES), U32),
        compiler_params=pltpu.CompilerParams(
            dimension_semantics=("arbitrary",), vmem_limit_bytes=VMEM_LIMIT_BYTES),
        name="gmm",
    )(block_e, next_e, slot, nblk, xs, w1, w3, w2)


def _final_kernel(destc_ref, destn_ref, gt_ref, x1_ref, h2p_ref, ada_ref, ws1_ref, ws3_ref, ws2_ref,
                  g_ref, b_ref, ys_hbm, o_ref, gbuf, gb_ref, sem):
    i = pl.program_id(0)
    last = pl.num_programs(0) - 1
    tm, d = x1_ref.shape
    half = d // 2
    n_lane_tiles = d // LANES
    slot = lax.rem(i, 2)
    other = 1 - slot

    def start_rows(dest_ref, s, t, j):
        for k in range(TOP_K):
            pltpu.make_async_copy(ys_hbm.at[_tile_rows(dest_ref[k, t])], gbuf.at[s, k, _tile_rows(t)],
                                  sem.at[s, k]).start(priority=(j * TOP_K + k) % 2)

    def wait_rows(s):
        for k in range(TOP_K):
            pltpu.make_async_copy(ys_hbm.at[pl.ds(0, tm * SUBLANES)], gbuf.at[s, k], sem.at[s, k]).wait()

    @pl.when(i == 0)
    def _():
        def group(g, carry):
            base = pl.multiple_of(g * ROW_DMA_GROUP, ROW_DMA_GROUP)
            for j in range(ROW_DMA_GROUP):
                start_rows(destc_ref, 0, base + j, j)
            return carry
        lax.fori_loop(0, tm // ROW_DMA_GROUP, group, 0)

    wait_rows(slot)

    xa, xb = _unpack_bf16_pair(_tiles_to_rows(h2p_ref, tm))
    xa = xa.astype(BF16)
    xb = xb.astype(BF16)
    h1 = _dot(xa, ws1_ref[:half, :]) + _dot(xb, ws1_ref[half:, :])
    h3 = _dot(xa, ws3_ref[:half, :]) + _dot(xb, ws3_ref[half:, :])
    shared = _dot((_silu(h1) * h3).astype(BF16), ws2_ref[...])

    for k in range(TOP_K):
        gb_ref[k] = jnp.broadcast_to(gt_ref[:, k:k + 1], (tm, LANES))
    gate2 = ada_ref[0, 5:6, :]

    per_chunk = tm // FIN_ISSUE_CHUNKS
    total = jnp.zeros((tm, LANES), F32)
    for c in range(SUBLANES):
        ra = jnp.zeros((tm, LANES), F32)
        rb = jnp.zeros((tm, LANES), F32)
        for k in range(TOP_K):
            ya, yb = _unpack_bf16_pair(gbuf[slot, k, pl.ds(c, tm, stride=SUBLANES), :])
            ra = ra + ya * gb_ref[k]
            rb = rb + yb * gb_ref[k]
        for r, cs in ((ra, slice(c * LANES, (c + 1) * LANES)), (rb, slice(half + c * LANES, half + (c + 1) * LANES))):
            y = DEEPNORM_ALPHA * x1_ref[:, cs] + gate2[:, cs] * (shared[:, cs] + r)
            o_ref[:, cs] = y
            total = total + y
        if c < FIN_ISSUE_CHUNKS:
            for t in range(c * per_chunk, (c + 1) * per_chunk):
                start_rows(destn_ref, other, t, t)
    mu = jnp.sum(total, axis=-1, keepdims=True) / d
    sq = jnp.zeros((tm, LANES), F32)
    for c in range(n_lane_tiles):
        yc = o_ref[:, c * LANES:(c + 1) * LANES] - mu
        sq = sq + yc * yc
    inv = lax.rsqrt(jnp.sum(sq, axis=-1, keepdims=True) / d + LN_EPS)
    for c in range(n_lane_tiles):
        cs = slice(c * LANES, (c + 1) * LANES)
        o_ref[:, cs] = (o_ref[:, cs] - mu) * inv * g_ref[:, cs] + b_ref[:, cs]

    @pl.when(i == last)
    def _():
        wait_rows(other)


def _final_call(dest, gates_t, x1, h2p, ada3, ws1_b, ws3_b, ws2_b, ln2_g, ln2_b, ys, seq):
    t, d = x1.shape
    ds_ = ws1_b.shape[1]
    tm = FIN_TM
    n = t // tm
    per_seq = seq // tm
    const2 = lambda i: (0, 0)
    return pl.pallas_call(
        _final_kernel,
        grid=(n,),
        in_specs=[
            pl.BlockSpec((TOP_K, tm), lambda i: (0, i), memory_space=pltpu.SMEM),
            pl.BlockSpec((TOP_K, tm), lambda i: (0, jnp.minimum(i + 1, n - 1)), memory_space=pltpu.SMEM),
            pl.BlockSpec((tm, TOP_K), lambda i: (i, 0)),
            pl.BlockSpec((tm, d), lambda i: (i, 0)),
            pl.BlockSpec((tm * SUBLANES, LANES), lambda i: (i, 0)),
            pl.BlockSpec((1, 6, d), lambda i: (i // per_seq, 0, 0)),
            pl.BlockSpec((d, ds_), const2),
            pl.BlockSpec((d, ds_), const2),
            pl.BlockSpec((ds_, d), const2),
            pl.BlockSpec((1, d), const2),
            pl.BlockSpec((1, d), const2),
            pl.BlockSpec(memory_space=pl.ANY),
        ],
        out_specs=pl.BlockSpec((tm, d), lambda i: (i, 0)),
        out_shape=jax.ShapeDtypeStruct((t, d), F32),
        scratch_shapes=[
            pltpu.VMEM((2, TOP_K, tm * SUBLANES, LANES), U32),
            pltpu.VMEM((TOP_K, tm, LANES), F32),
            pltpu.SemaphoreType.DMA((2, TOP_K)),
        ],
        compiler_params=pltpu.CompilerParams(
            dimension_semantics=("arbitrary",), vmem_limit_bytes=VMEM_LIMIT_BYTES),
        name="final",
    )(dest, dest, gates_t, x1, h2p, ada3, ws1_b, ws3_b, ws2_b, ln2_g.reshape(1, d), ln2_b.reshape(1, d), ys)


def kernel(x, c, w_ada, b_ada, w_in, b_in, ln_v_g, ln_v_b, w_spatial, b_spatial, w_pool, b_pool, pool_scale, w_out, ln1_g, ln1_b, w_router, router_bias, w1, w3, w2, ws1, ws3, ws2, ln2_g, ln2_b):
    bsz, seq, d = x.shape
    t = bsz * seq
    e = w_router.shape[-1]
    assert w_ada.shape[0] == DEPTH
    l = 0

    ada3 = _ada_call(c, w_ada[l], b_ada[l]).reshape(bsz, 6, d)

    cat = _mix_call(x, ada3, w_in[l].astype(BF16), b_in[l], ln_v_g[l], ln_v_b[l], w_spatial[l],
                    b_spatial[l].T, w_pool[l].astype(BF16), b_pool[l], pool_scale[l])

    wr_t = w_router[l].T
    wr_hi = wr_t.astype(BF16)
    wr_lo = (wr_t - wr_hi.astype(F32)).astype(BF16)
    x1, h2p, logits_t = _out_call(cat.reshape(t, d), x.reshape(t, d), ada3, w_out[l].astype(BF16),
                                  ln1_g[l], ln1_b[l], wr_hi, wr_lo, seq)

    idx, gates, rank, cnt = _route_call(logits_t, router_bias[l])

    tb = GMM_TB
    nb = (t * TOP_K) // tb + e
    dest, last, be, nxt, par, nblk = _plan_call(cnt, idx, rank, nb, tb)

    xs = _dispatch_call(dest, h2p, _padzero_call(last[:, 0], nb, tb))

    ys = _gmm_call(be[0], nxt[:, 0, :], par[0], nblk[0, :1], xs, w1.reshape(w1.shape[1:]),
                   w3.reshape(w3.shape[1:]), w2.reshape(w2.shape[1:]))

    out = _final_call(dest, gates.T, x1, h2p, ada3, ws1[l].astype(BF16), ws3[l].astype(BF16),
                      ws2[l].astype(BF16), ln2_g[l], ln2_b[l], ys, seq)
    return out.reshape(bsz, seq, d)
```

```python
import functools

import jax
import jax.numpy as jnp
from jax import lax
from jax.experimental import pallas as pl
from jax.experimental.pallas import tpu as pltpu

F32 = jnp.float32
BF16 = jnp.bfloat16
U32 = jnp.uint32
I32 = jnp.int32

N_HEADS_A = 8
GMLP_BLOCK = 128
CHUNK = 64
POOL_WINDOWS = (2, 4, 8, 16)
POOL_HISTORY = 16
N_EXPERT_GROUPS = 8
TOPK_GROUPS = 4
TOP_K = 8
ROUTED_SCALE = 2.5
LN_EPS = 1e-5
DEPTH = 1
DEEPNORM_ALPHA = (2.0 * DEPTH) ** 0.25

LANES = 128
SUBLANES = 8
VMEM_LIMIT_BYTES = 56 * 1024 * 1024
ADA_TN = 1024
MIX_TS = 512
MIX_SUB = 256
OUT_TM = 512
OUT_SUB = 256
ROUTE_TR = 256
PLAN_TR = 512
GMM_TB = 256
DISP_TM = 512
FIN_TM = 256
ROW_DMA_GROUP = 16
PADZERO_GROUP = 8
WEIGHT_SLOTS = 3
WEIGHT_DMA_PRIORITY = 1
CAST_ROWS = 256


def _dot(a, b):
    return jnp.dot(a, b, preferred_element_type=F32)


def _split_bf16(a):
    hi = a.astype(BF16)
    lo = (a - hi.astype(F32)).astype(BF16)
    return hi, lo


def _gelu(x):
    return 0.5 * x * (1.0 + lax.erf(x * 0.7071067811865476))


def _silu(x):
    return x * jax.nn.sigmoid(x)


def _pack_bf16_pair(a, b):
    au = pltpu.bitcast(a.astype(BF16).astype(F32), U32)
    bu = pltpu.bitcast(b.astype(BF16).astype(F32), U32)
    return (au & jnp.uint32(0xFFFF0000)) | (bu >> 16)


def _unpack_bf16_pair(w):
    a = pltpu.bitcast(w & jnp.uint32(0xFFFF0000), F32)
    b = pltpu.bitcast(w << 16, F32)
    return a, b


def _tiles_to_rows(ref, n):
    return jnp.concatenate([ref[pl.ds(c, n, stride=SUBLANES), :] for c in range(SUBLANES)], axis=1)


def _rows_to_tiles(ref, value):
    n = value.shape[0]
    for c in range(SUBLANES):
        ref[pl.ds(c, n, stride=SUBLANES), :] = value[:, c * LANES:(c + 1) * LANES]


def _ada_kernel(c_ref, w_ref, b_ref, o_ref):
    s = _silu(c_ref[...])
    sh, sl = _split_bf16(s)
    wh, wl = _split_bf16(w_ref[...])
    o_ref[...] = _dot(sh, wh) + _dot(sl, wh) + _dot(sh, wl) + b_ref[...]


def _ada_call(c, w_ada, b_ada):
    bsz, d = c.shape
    n = w_ada.shape[1]
    return pl.pallas_call(
        _ada_kernel,
        grid=(n // ADA_TN,),
        in_specs=[
            pl.BlockSpec((bsz, d), lambda j: (0, 0)),
            pl.BlockSpec((d, ADA_TN), lambda j: (0, j)),
            pl.BlockSpec((1, ADA_TN), lambda j: (0, j)),
        ],
        out_specs=pl.BlockSpec((bsz, ADA_TN), lambda j: (0, j)),
        out_shape=jax.ShapeDtypeStruct((bsz, n), F32),
        compiler_params=pltpu.CompilerParams(
            dimension_semantics=("arbitrary",), vmem_limit_bytes=VMEM_LIMIT_BYTES),
        name="ada",
    )(c, w_ada, b_ada.reshape(1, n))


def _mix_kernel(x_ref, ada_ref, win_ref, bin_ref, lng_ref, lnb_ref, wsp_ref, bsp_ref,
                wpool_ref, bpool_ref, pscale_ref, o_ref, carry_ref, *, d_a, d_g):
    j = pl.program_id(1)
    shift = ada_ref[0, 0:1, :]
    scale = ada_ref[0, 1:2, :]
    row = lax.broadcasted_iota(I32, (GMLP_BLOCK, GMLP_BLOCK), 0)
    col = lax.broadcasted_iota(I32, (GMLP_BLOCK, GMLP_BLOCK), 1)
    causal = (col // CHUNK) <= (row // CHUNK)
    hd = d_a // N_HEADS_A

    @pl.when(j == 0)
    def _():
        carry_ref[...] = jnp.zeros_like(carry_ref)

    ts = MIX_SUB
    for r0 in range(0, x_ref.shape[1], ts):
        h = (x_ref[0, r0:r0 + ts, :] * (1.0 + scale) + shift).astype(BF16)
        proj = _dot(h, win_ref[...]) + bin_ref[...]

        u = _gelu(proj[:, :d_a])
        v = _gelu(proj[:, d_a:2 * d_a])
        mu = jnp.mean(v, axis=-1, keepdims=True)
        vc = v - mu
        var = jnp.mean(vc * vc, axis=-1, keepdims=True)
        vn = (vc * lax.rsqrt(var + LN_EPS) * lng_ref[...] + lnb_ref[...]).astype(BF16)
        for head in range(N_HEADS_A):
            w = jnp.where(causal, wsp_ref[head], 0.0).astype(BF16)
            bias = bsp_ref[:, head:head + 1]
            cs = slice(head * hd, (head + 1) * hd)
            for n in range(ts // GMLP_BLOCK):
                rs = slice(n * GMLP_BLOCK, (n + 1) * GMLP_BLOCK)
                mixed = _dot(w, vn[rs, cs]) + bias
                o_ref[0, r0 + n * GMLP_BLOCK:r0 + (n + 1) * GMLP_BLOCK, cs] = (u[rs, cs] * mixed).astype(BF16)

        z = proj[:, 2 * d_a:]
        ext = jnp.concatenate([carry_ref[...], z], axis=0)
        carry_ref[...] = z[ts - POOL_HISTORY:, :]
        tpos = j * x_ref.shape[1] + r0 + lax.broadcasted_iota(I32, (ts, 1), 0)
        for g, win in enumerate(POOL_WINDOWS):
            gs = slice(g * d_g, (g + 1) * d_g)
            s = ext[:, gs]
            sh = 1
            while sh < win:
                s = s + pltpu.roll(s, sh, axis=0)
                sh *= 2
            cnt = jnp.minimum(tpos + 1, win).astype(F32)
            pooled = s[POOL_HISTORY:, :] / cnt - z[:, gs]
            y = _dot(pooled.astype(BF16), wpool_ref[g]) + bpool_ref[g:g + 1, :]
            o_ref[0, r0:r0 + ts, d_a + g * d_g:d_a + (g + 1) * d_g] = (y * pscale_ref[:, gs]).astype(BF16)


def _mix_call(x, ada3, w_in_b, b_in, ln_v_g, ln_v_b, w_spatial, b_spatial_t, w_pool_b, b_pool, pool_scale):
    bsz, seq, d = x.shape
    n_proj = w_in_b.shape[1]
    d_a = ln_v_g.shape[0]
    d_b = pool_scale.shape[0]
    n_g, d_g, _ = w_pool_b.shape
    ts = MIX_TS
    const2 = lambda b, j: (0, 0)
    const3 = lambda b, j: (0, 0, 0)
    return pl.pallas_call(
        functools.partial(_mix_kernel, d_a=d_a, d_g=d_g),
        grid=(bsz, seq // ts),
        in_specs=[
            pl.BlockSpec((1, ts, d), lambda b, j: (b, j, 0)),
            pl.BlockSpec((1, 6, d), lambda b, j: (b, 0, 0)),
            pl.BlockSpec((d, n_proj), const2, pipeline_mode=pl.Buffered(1)),
            pl.BlockSpec((1, n_proj), const2),
            pl.BlockSpec((1, d_a), const2),
            pl.BlockSpec((1, d_a), const2),
            pl.BlockSpec((N_HEADS_A, GMLP_BLOCK, GMLP_BLOCK), const3),
            pl.BlockSpec((GMLP_BLOCK, N_HEADS_A), const2),
            pl.BlockSpec((n_g, d_g, d_g), const3),
            pl.BlockSpec((n_g, d_g), const2),
            pl.BlockSpec((1, d_b), const2),
        ],
        out_specs=pl.BlockSpec((1, ts, d_a + d_b), lambda b, j: (b, j, 0)),
        out_shape=jax.ShapeDtypeStruct((bsz, seq, d_a + d_b), BF16),
        scratch_shapes=[pltpu.VMEM((POOL_HISTORY, d_b), F32)],
        compiler_params=pltpu.CompilerParams(
            dimension_semantics=("arbitrary", "arbitrary"), vmem_limit_bytes=VMEM_LIMIT_BYTES),
        name="mix",
    )(x, ada3, w_in_b, b_in.reshape(1, n_proj), ln_v_g.reshape(1, d_a), ln_v_b.reshape(1, d_a),
      w_spatial, b_spatial_t, w_pool_b, b_pool, pool_scale.reshape(1, d_b))


def _layer_norm_rows(y, g, b):
    mu = jnp.mean(y, axis=-1, keepdims=True)
    yc = y - mu
    var = jnp.mean(yc * yc, axis=-1, keepdims=True)
    return yc * lax.rsqrt(var + LN_EPS) * g + b


def _out_kernel(cat_ref, x_ref, ada_ref, wout_ref, g_ref, b_ref, wrh_ref, wrl_ref,
                x1_ref, h2p_ref, lgt_ref):
    tm, d = x_ref.shape
    half = d // 2
    gate1 = ada_ref[0, 2:3, :]
    nt = (((1,), (1,)), ((), ()))
    for r0 in range(0, tm, OUT_SUB):
        rows = slice(r0, r0 + OUT_SUB)
        mix = _dot(cat_ref[rows, :], wout_ref[...])
        x1 = _layer_norm_rows(DEEPNORM_ALPHA * x_ref[rows, :] + gate1 * mix, g_ref[...], b_ref[...])
        x1_ref[rows, :] = x1
        h2 = x1 * (1.0 + ada_ref[0, 4:5, :]) + ada_ref[0, 3:4, :]
        _rows_to_tiles(h2p_ref.at[pl.ds(r0 * SUBLANES, OUT_SUB * SUBLANES)],
                       _pack_bf16_pair(h2[:, :half], h2[:, half:]))
        hh, hl = _split_bf16(h2)
        wrh = wrh_ref[...]
        lgt_ref[:, rows] = (lax.dot_general(wrh, hh, nt, preferred_element_type=F32)
                            + lax.dot_general(wrh, hl, nt, preferred_element_type=F32)
                            + lax.dot_general(wrl_ref[...], hh, nt, preferred_element_type=F32))


def _out_call(cat, x2d, ada3, w_out_b, ln1_g, ln1_b, wr_hi_t, wr_lo_t, seq):
    t, d = x2d.shape
    e = wr_hi_t.shape[0]
    tm = OUT_TM
    per_seq = seq // tm
    const2 = lambda i: (0, 0)
    once = pl.Buffered(1)
    return pl.pallas_call(
        _out_kernel,
        grid=(t // tm,),
        in_specs=[
            pl.BlockSpec((tm, d), lambda i: (i, 0)),
            pl.BlockSpec((tm, d), lambda i: (i, 0)),
            pl.BlockSpec((1, 6, d), lambda i: (i // per_seq, 0, 0)),
            pl.BlockSpec((d, d), const2, pipeline_mode=once),
            pl.BlockSpec((1, d), const2),
            pl.BlockSpec((1, d), const2),
            pl.BlockSpec((e, d), const2, pipeline_mode=once),
            pl.BlockSpec((e, d), const2, pipeline_mode=once),
        ],
        out_specs=[
            pl.BlockSpec((tm, d), lambda i: (i, 0)),
            pl.BlockSpec((tm * SUBLANES, LANES), lambda i: (i, 0)),
            pl.BlockSpec((e, tm), lambda i: (0, i)),
        ],
        out_shape=[
            jax.ShapeDtypeStruct((t, d), F32),
            jax.ShapeDtypeStruct((t * SUBLANES, LANES), U32),
            jax.ShapeDtypeStruct((e, t), F32),
        ],
        compiler_params=pltpu.CompilerParams(
            dimension_semantics=("arbitrary",), vmem_limit_bytes=VMEM_LIMIT_BYTES),
        name="out",
    )(cat, x2d, ada3, w_out_b, ln1_g.reshape(1, d), ln1_b.reshape(1, d), wr_hi_t, wr_lo_t)


def _route_kernel(lg_ref, bias_ref, idx_ref, gate_ref, rank_ref, cnt_ref, run_ref, gs_ref, keep_ref):
    i = pl.program_id(0)
    e, tr = lg_ref.shape
    per_group = e // N_EXPERT_GROUPS
    neg = -jnp.inf

    @pl.when(i == 0)
    def _():
        run_ref[...] = jnp.zeros_like(run_ref)

    scores = jax.nn.sigmoid(lg_ref[...])
    sel = scores + bias_ref[...]
    rowi = lax.broadcasted_iota(I32, (e, tr), 0)

    for g in range(N_EXPERT_GROUPS):
        rs = slice(g * per_group, (g + 1) * per_group)
        blk = sel[rs]
        ri = lax.broadcasted_iota(I32, (per_group, tr), 0) + g * per_group
        m1 = jnp.max(blk, axis=0, keepdims=True)
        i1 = jnp.min(jnp.where(blk == m1, ri, e), axis=0, keepdims=True)
        m2 = jnp.max(jnp.where(ri == i1, neg, blk), axis=0, keepdims=True)
        gs_ref[g:g + 1, :] = m1 + m2
    gmat = gs_ref[...]
    gi = lax.broadcasted_iota(I32, gmat.shape, 0)
    beaten = jnp.zeros(gmat.shape, I32)
    for g in range(N_EXPERT_GROUPS):
        gj = gs_ref[g:g + 1, :]
        beats = jnp.where(gj > gmat, 1, jnp.where((gj == gmat) & (gi > g), 1, 0))
        beaten = beaten + beats
    keep_ref[...] = jnp.where(beaten < TOPK_GROUPS, 1.0, 0.0)
    ekeep = jnp.concatenate(
        [jnp.broadcast_to(keep_ref[g:g + 1, :], (per_group, tr)) for g in range(N_EXPERT_GROUPS)], axis=0)
    masked = jnp.where(ekeep > 0.5, sel, neg)

    idxs, tops = [], []
    for k in range(TOP_K):
        m = jnp.max(masked, axis=0, keepdims=True)
        ik = jnp.min(jnp.where(masked == m, rowi, e), axis=0, keepdims=True)
        hit = rowi == ik
        tops.append(jnp.sum(jnp.where(hit, scores, 0.0), axis=0, keepdims=True))
        masked = jnp.where(hit, neg, masked)
        idxs.append(ik)
        idx_ref[k:k + 1, :] = ik
    total = tops[0]
    for k in range(1, TOP_K):
        total = total + tops[k]
    for k in range(TOP_K):
        gate_ref[k:k + 1, :] = tops[k] / total * ROUTED_SCALE

    chosen = jnp.zeros((e, tr), F32)
    for ik in idxs:
        chosen = chosen + jnp.where(rowi == ik, 1.0, 0.0)
    before = lax.broadcasted_iota(I32, (tr, tr), 0) < lax.broadcasted_iota(I32, (tr, tr), 1)
    upper = jnp.where(before, 1.0, 0.0).astype(BF16)
    pos = _dot(chosen.astype(BF16), upper) + run_ref[...]
    for k, ik in enumerate(idxs):
        rank_ref[k:k + 1, :] = jnp.sum(jnp.where(rowi == ik, pos, 0.0), axis=0, keepdims=True).astype(I32)
    run = run_ref[...] + jnp.sum(chosen, axis=1, keepdims=True)
    run_ref[...] = run
    cnt_ref[...] = jnp.broadcast_to(run, cnt_ref.shape).astype(I32)


def _route_call(logits_t, router_bias):
    e, t = logits_t.shape
    tr = ROUTE_TR
    tok = lambda i: (0, i)
    return pl.pallas_call(
        _route_kernel,
        grid=(t // tr,),
        in_specs=[pl.BlockSpec((e, tr), tok), pl.BlockSpec((e, 1), lambda i: (0, 0))],
        out_specs=[
            pl.BlockSpec((TOP_K, tr), tok),
            pl.BlockSpec((TOP_K, tr), tok),
            pl.BlockSpec((TOP_K, tr), tok),
            pl.BlockSpec((e, LANES), lambda i: (0, 0)),
        ],
        out_shape=[
            jax.ShapeDtypeStruct((TOP_K, t), I32),
            jax.ShapeDtypeStruct((TOP_K, t), F32),
            jax.ShapeDtypeStruct((TOP_K, t), I32),
            jax.ShapeDtypeStruct((e, LANES), I32),
        ],
        scratch_shapes=[pltpu.VMEM((e, 1), F32), pltpu.VMEM((N_EXPERT_GROUPS, tr), F32),
                        pltpu.VMEM((N_EXPERT_GROUPS, tr), F32)],
        compiler_params=pltpu.CompilerParams(
            dimension_semantics=("arbitrary",), vmem_limit_bytes=VMEM_LIMIT_BYTES),
        name="route",
    )(logits_t, router_bias.reshape(e, 1))


def _plan_kernel(cnt_ref, idx_ref, rank_ref, dest_ref, last_ref, be_ref, nxt_ref, par_ref, nblk_ref,
                 pst_ref, *, tb):
    e = cnt_ref.shape[0]
    nb = be_ref.shape[1]
    tr = idx_ref.shape[1]

    @pl.when(pl.program_id(0) == 0)
    def _():
        shift = tb.bit_length() - 1
        nblocks = jnp.right_shift(cnt_ref[...] + (tb - 1), shift).astype(F32)
        lower = jnp.where(lax.broadcasted_iota(I32, (e, e), 1) <= lax.broadcasted_iota(I32, (e, e), 0), 1.0, 0.0)
        bend = _dot(lower.astype(BF16), nblocks.astype(BF16))
        nblk = bend[e - 1:e, :]
        pst_ref[...] = ((bend - nblocks) * tb)[:, 0:1]
        last_ref[...] = jnp.where(nblocks > 0, bend - 1.0, -1.0).astype(I32)
        nblk_ref[...] = jnp.broadcast_to(nblk, nblk_ref.shape).astype(I32)
        blk = lax.broadcasted_iota(I32, (e, nb), 1).astype(F32)
        exp = lax.broadcasted_iota(I32, (e, nb), 0).astype(F32)
        be = jnp.minimum(jnp.sum(jnp.where(bend[:, 0:1] <= blk, 1.0, 0.0), axis=0, keepdims=True), e - 1.0)
        later = jnp.where((nblocks[:, 0:1] > 0) & (exp > be), exp, float(e))
        nxt = jnp.min(later, axis=0, keepdims=True)
        chain = [nxt]
        for _ in range(WEIGHT_SLOTS - 1):
            later = jnp.where((nblocks[:, 0:1] > 0) & (exp > chain[-1]), exp, float(e))
            chain.append(jnp.min(later, axis=0, keepdims=True))
        earlier = jnp.sum(jnp.where((nblocks[:, 0:1] > 0) & (exp < be), 1.0, 0.0), axis=0, keepdims=True)
        slot = earlier - WEIGHT_SLOTS * jnp.floor((earlier + 0.5) / WEIGHT_SLOTS)
        be_ref[...] = jnp.broadcast_to(be, be_ref.shape).astype(I32)
        for j, nx in enumerate(chain):
            nxt_ref[j] = jnp.broadcast_to(jnp.where(nx >= e, -1.0, nx), be_ref.shape).astype(I32)
        par_ref[...] = jnp.broadcast_to(slot, par_ref.shape).astype(I32)

    rowi = lax.broadcasted_iota(I32, (e, tr), 0)
    pst = pst_ref[...]
    for k in range(TOP_K):
        start = jnp.sum(jnp.where(rowi == idx_ref[k:k + 1, :], pst, 0.0), axis=0, keepdims=True)
        dest_ref[k:k + 1, :] = start.astype(I32) + rank_ref[k:k + 1, :]


def _plan_call(cnt, idx, rank, nb, tb):
    e = cnt.shape[0]
    t = idx.shape[1]
    tr = PLAN_TR
    tok = lambda i: (0, i)
    const = lambda i: (0, 0)
    return pl.pallas_call(
        functools.partial(_plan_kernel, tb=tb),
        grid=(t // tr,),
        in_specs=[pl.BlockSpec((e, LANES), const), pl.BlockSpec((TOP_K, tr), tok), pl.BlockSpec((TOP_K, tr), tok)],
        out_specs=[
            pl.BlockSpec((TOP_K, tr), tok),
            pl.BlockSpec((e, LANES), const),
            pl.BlockSpec((SUBLANES, nb), const),
            pl.BlockSpec((WEIGHT_SLOTS, SUBLANES, nb), lambda i: (0, 0, 0)),
            pl.BlockSpec((SUBLANES, nb), const),
            pl.BlockSpec((SUBLANES, LANES), const),
        ],
        out_shape=[
            jax.ShapeDtypeStruct((TOP_K, t), I32),
            jax.ShapeDtypeStruct((e, LANES), I32),
            jax.ShapeDtypeStruct((SUBLANES, nb), I32),
            jax.ShapeDtypeStruct((WEIGHT_SLOTS, SUBLANES, nb), I32),
            jax.ShapeDtypeStruct((SUBLANES, nb), I32),
            jax.ShapeDtypeStruct((SUBLANES, LANES), I32),
        ],
        scratch_shapes=[pltpu.VMEM((e, 1), F32)],
        compiler_params=pltpu.CompilerParams(dimension_semantics=("arbitrary",)),
        name="plan",
    )(cnt, idx, rank)


def _padzero_kernel(last_ref, xs_out, zbuf, sem):
    n_groups = last_ref.shape[0] // PADZERO_GROUP
    rows = zbuf.shape[0]
    zbuf[...] = jnp.zeros_like(zbuf)

    def for_group(g, action):
        for j in range(PADZERO_GROUP):
            blk = last_ref[g * PADZERO_GROUP + j]

            @pl.when(blk >= 0)
            def _():
                action(pltpu.make_async_copy(zbuf, xs_out.at[pl.ds(pl.multiple_of(blk * rows, rows), rows)],
                                             sem.at[lax.rem(g, 2)]))

    def body(g, carry):
        for_group(g, lambda cp: cp.start())

        @pl.when(g > 0)
        def _():
            for_group(g - 1, lambda cp: cp.wait())
        return carry

    lax.fori_loop(0, n_groups, body, 0)
    for_group(n_groups - 1, lambda cp: cp.wait())


def _padzero_call(last_blk, nb, tb):
    grid_spec = pltpu.PrefetchScalarGridSpec(
        num_scalar_prefetch=1,
        grid=(1,),
        in_specs=[],
        out_specs=pl.BlockSpec(memory_space=pl.ANY),
        scratch_shapes=[pltpu.VMEM((tb * SUBLANES, LANES), U32), pltpu.SemaphoreType.DMA((2,))],
    )
    assert last_blk.shape[0] % PADZERO_GROUP == 0
    return pl.pallas_call(
        _padzero_kernel,
        grid_spec=grid_spec,
        out_shape=jax.ShapeDtypeStruct((nb * tb * SUBLANES, LANES), U32),
        compiler_params=pltpu.CompilerParams(dimension_semantics=("arbitrary",)),
        name="padzero",
    )(last_blk)


def _tile_rows(r):
    return pl.ds(pl.multiple_of(r * SUBLANES, SUBLANES), SUBLANES)


def _dispatch_kernel(dest_ref, h_ref, xs_in, xs_out, sem):
    del xs_in
    tm = dest_ref.shape[1]

    def group(g, carry):
        base = pl.multiple_of(g * ROW_DMA_GROUP, ROW_DMA_GROUP)
        for j in range(ROW_DMA_GROUP):
            for k in range(TOP_K):
                pltpu.make_async_copy(h_ref.at[_tile_rows(base + j)], xs_out.at[_tile_rows(dest_ref[k, base + j])],
                                      sem).start(priority=(j * TOP_K + k) % 2)
        return carry

    lax.fori_loop(0, tm // ROW_DMA_GROUP, group, 0)
    for k in range(TOP_K):
        pltpu.make_async_copy(h_ref, xs_out.at[pl.ds(0, tm * SUBLANES)], sem).wait()


def _dispatch_call(dest, h2p, xs):
    t = dest.shape[1]
    tm = DISP_TM
    return pl.pallas_call(
        _dispatch_kernel,
        grid=(t // tm,),
        in_specs=[
            pl.BlockSpec((TOP_K, tm), lambda i: (0, i), memory_space=pltpu.SMEM),
            pl.BlockSpec((tm * SUBLANES, LANES), lambda i: (i, 0)),
            pl.BlockSpec(memory_space=pl.ANY),
        ],
        out_specs=pl.BlockSpec(memory_space=pl.ANY),
        out_shape=jax.ShapeDtypeStruct(xs.shape, xs.dtype),
        scratch_shapes=[pltpu.SemaphoreType.DMA],
        input_output_aliases={2: 0},
        compiler_params=pltpu.CompilerParams(
            dimension_semantics=("arbitrary",), vmem_limit_bytes=VMEM_LIMIT_BYTES),
        name="dispatch",
    )(dest, h2p, xs)


def _gmm_kernel(be_ref, nxt_ref, par_ref, nb_ref, x_ref, w1_hbm, w3_hbm, w2_hbm, o_ref,
                st1, st3, st2, sem, wb1, wb3, wb2):
    i = pl.program_id(0)
    half = st1.shape[1] // 2
    tb = x_ref.shape[0] // SUBLANES
    nblk = nb_ref[0]

    def weight_copies(ex, s):
        return (pltpu.make_async_copy(w1_hbm.at[ex], st1.at[s], sem.at[s, 0]),
                pltpu.make_async_copy(w3_hbm.at[ex], st3.at[s], sem.at[s, 1]),
                pltpu.make_async_copy(w2_hbm.at[ex], st2.at[s], sem.at[s, 2]))

    def start_weights(ex, s):
        @pl.when(ex >= 0)
        def _():
            for cp in weight_copies(ex, s):
                cp.start(priority=WEIGHT_DMA_PRIORITY)

    @pl.when(i == 0)
    def _():
        start_weights(be_ref[0], 0)
        for j in range(WEIGHT_SLOTS - 1):
            start_weights(nxt_ref[j, 0], j + 1)

    prev = jnp.maximum(i - 1, 0)
    prev_fresh = jnp.logical_or(i == 1, be_ref[prev] != be_ref[jnp.maximum(i - 2, 0)])

    @pl.when(jnp.logical_and(jnp.logical_and(i >= 1, i - 1 < nblk), prev_fresh))
    def _():
        start_weights(nxt_ref[WEIGHT_SLOTS - 1, prev], par_ref[prev])

    active = i < nblk
    fresh = jnp.logical_or(i == 0, be_ref[i] != be_ref[jnp.maximum(i - 1, 0)])

    @pl.when(jnp.logical_and(active, fresh))
    def _():
        s = par_ref[i]
        for cp in weight_copies(be_ref[i], s):
            cp.wait()

        xa, xb = _unpack_bf16_pair(_tiles_to_rows(x_ref, tb))
        xs = (xa.astype(BF16), xb.astype(BF16))
        n_chunks = st1.shape[1] // CAST_ROWS
        per_half = n_chunks // 2
        rows2 = st2.shape[1] // n_chunks
        h1 = None
        h3 = None
        for c in range(n_chunks):
            rs = slice(c * CAST_ROWS, (c + 1) * CAST_ROWS)
            wb1[rs, :] = st1[s, rs, :].astype(BF16)
            wb3[rs, :] = st3[s, rs, :].astype(BF16)
            rs2 = slice(c * rows2, (c + 1) * rows2)
            wb2[rs2, :] = st2[s, rs2, :].astype(BF16)
            xc = xs[c // per_half][:, (c % per_half) * CAST_ROWS:(c % per_half + 1) * CAST_ROWS]
            p1 = _dot(xc, wb1[rs, :])
            p3 = _dot(xc, wb3[rs, :])
            h1 = p1 if h1 is None else h1 + p1
            h3 = p3 if h3 is None else h3 + p3
        y = _dot((_silu(h1) * h3).astype(BF16), wb2[...])
        _rows_to_tiles(o_ref, _pack_bf16_pair(y[:, :half], y[:, half:]))

    @pl.when(jnp.logical_and(active, jnp.logical_not(fresh)))
    def _():
        xa, xb = _unpack_bf16_pair(_tiles_to_rows(x_ref, tb))
        xa = xa.astype(BF16)
        xb = xb.astype(BF16)
        h1 = _dot(xa, wb1[:half, :]) + _dot(xb, wb1[half:, :])
        h3 = _dot(xa, wb3[:half, :]) + _dot(xb, wb3[half:, :])
        y = _dot((_silu(h1) * h3).astype(BF16), wb2[...])
        _rows_to_tiles(o_ref, _pack_bf16_pair(y[:, :half], y[:, half:]))

    @pl.when(i == nblk)
    def _():
        o_ref[...] = jnp.zeros_like(o_ref)


def _gmm_call(block_e, next_e, slot, nblk, xs, w1, w3, w2):
    e, d, de = w1.shape
    assert d // 2 == SUBLANES * LANES
    tb = GMM_TB
    nb = block_e.shape[0]
    blk = (tb * SUBLANES, LANES)
    grid_spec = pltpu.PrefetchScalarGridSpec(
        num_scalar_prefetch=4,
        grid=(nb,),
        in_specs=[
            pl.BlockSpec(blk, lambda i, be, nx, pa, n: (jnp.minimum(i, n[0] - 1), 0)),
            pl.BlockSpec(memory_space=pl.ANY),
            pl.BlockSpec(memory_space=pl.ANY),
            pl.BlockSpec(memory_space=pl.ANY),
        ],
        out_specs=pl.BlockSpec(blk, lambda i, be, nx, pa, n: (jnp.minimum(i, n[0]), 0)),
        scratch_shapes=[
            pltpu.VMEM((WEIGHT_SLOTS, d, de), F32),
            pltpu.VMEM((WEIGHT_SLOTS, d, de), F32),
            pltpu.VMEM((WEIGHT_SLOTS, de, d), F32),
            pltpu.SemaphoreType.DMA((WEIGHT_SLOTS, 3)),
            pltpu.VMEM((d, de), BF16),
            pltpu.VMEM((d, de), BF16),
            pltpu.VMEM((de, d), BF16),
        ],
    )
    return pl.pallas_call(
        _gmm_kernel,
        grid_spec=grid_spec,
        out_shape=jax.ShapeDtypeStruct((nb * tb * SUBLANES, LANES), U32),
        compiler_params=pltpu.CompilerParams(
            dimension_semantics=("arbitrary",), vmem_limit_bytes=VMEM_LIMIT_BYTES),
        name="gmm",
    )(block_e, next_e, slot, nblk, xs, w1, w3, w2)


def _final_kernel(destc_ref, destn_ref, gt_ref, x1_ref, h2p_ref, ada_ref, ws1_ref, ws3_ref, ws2_ref,
                  g_ref, b_ref, ys_hbm, o_ref, gbuf, gb_ref, sem):
    i = pl.program_id(0)
    last = pl.num_programs(0) - 1
    tm, d = x1_ref.shape
    half = d // 2
    n_lane_tiles = d // LANES
    slot = lax.rem(i, 2)
    other = 1 - slot

    def start_rows(dest_ref, s, t, j):
        for k in range(TOP_K):
            pltpu.make_async_copy(ys_hbm.at[_tile_rows(dest_ref[k, t])], gbuf.at[s, k, _tile_rows(t)],
                                  sem.at[s, k]).start(priority=(j * TOP_K + k) % 2)

    def wait_rows(s):
        for k in range(TOP_K):
            pltpu.make_async_copy(ys_hbm.at[pl.ds(0, tm * SUBLANES)], gbuf.at[s, k], sem.at[s, k]).wait()

    @pl.when(i == 0)
    def _():
        def group(g, carry):
            base = pl.multiple_of(g * ROW_DMA_GROUP, ROW_DMA_GROUP)
            for j in range(ROW_DMA_GROUP):
                start_rows(destc_ref, 0, base + j, j)
            return carry
        lax.fori_loop(0, tm // ROW_DMA_GROUP, group, 0)

    wait_rows(slot)

    for t in range(tm):
        start_rows(destn_ref, other, t, t)

    xa, xb = _unpack_bf16_pair(_tiles_to_rows(h2p_ref, tm))
    xa = xa.astype(BF16)
    xb = xb.astype(BF16)
    h1 = _dot(xa, ws1_ref[:half, :]) + _dot(xb, ws1_ref[half:, :])
    h3 = _dot(xa, ws3_ref[:half, :]) + _dot(xb, ws3_ref[half:, :])
    shared = _dot((_silu(h1) * h3).astype(BF16), ws2_ref[...])

    for k in range(TOP_K):
        gb_ref[k] = jnp.broadcast_to(gt_ref[:, k:k + 1], (tm, LANES))
    gate2 = ada_ref[0, 5:6, :]

    total = jnp.zeros((tm, LANES), F32)
    for c in range(SUBLANES):
        ra = jnp.zeros((tm, LANES), F32)
        rb = jnp.zeros((tm, LANES), F32)
        for k in range(TOP_K):
            ya, yb = _unpack_bf16_pair(gbuf[slot, k, pl.ds(c, tm, stride=SUBLANES), :])
            ra = ra + ya * gb_ref[k]
            rb = rb + yb * gb_ref[k]
        for r, cs in ((ra, slice(c * LANES, (c + 1) * LANES)), (rb, slice(half + c * LANES, half + (c + 1) * LANES))):
            y = DEEPNORM_ALPHA * x1_ref[:, cs] + gate2[:, cs] * (shared[:, cs] + r)
            o_ref[:, cs] = y
            total = total + y
    mu = jnp.sum(total, axis=-1, keepdims=True) / d
    sq = jnp.zeros((tm, LANES), F32)
    for c in range(n_lane_tiles):
        yc = o_ref[:, c * LANES:(c + 1) * LANES] - mu
        sq = sq + yc * yc
    inv = lax.rsqrt(jnp.sum(sq, axis=-1, keepdims=True) / d + LN_EPS)
    for c in range(n_lane_tiles):
        cs = slice(c * LANES, (c + 1) * LANES)
        o_ref[:, cs] = (o_ref[:, cs] - mu) * inv * g_ref[:, cs] + b_ref[:, cs]

    @pl.when(i == last)
    def _():
        wait_rows(other)


def _final_call(dest, gates_t, x1, h2p, ada3, ws1_b, ws3_b, ws2_b, ln2_g, ln2_b, ys, seq):
    t, d = x1.shape
    ds_ = ws1_b.shape[1]
    tm = FIN_TM
    n = t // tm
    per_seq = seq // tm
    const2 = lambda i: (0, 0)
    return pl.pallas_call(
        _final_kernel,
        grid=(n,),
        in_specs=[
            pl.BlockSpec((TOP_K, tm), lambda i: (0, i), memory_space=pltpu.SMEM),
            pl.BlockSpec((TOP_K, tm), lambda i: (0, jnp.minimum(i + 1, n - 1)), memory_space=pltpu.SMEM),
            pl.BlockSpec((tm, TOP_K), lambda i: (i, 0)),
            pl.BlockSpec((tm, d), lambda i: (i, 0)),
            pl.BlockSpec((tm * SUBLANES, LANES), lambda i: (i, 0)),
            pl.BlockSpec((1, 6, d), lambda i: (i // per_seq, 0, 0)),
            pl.BlockSpec((d, ds_), const2),
            pl.BlockSpec((d, ds_), const2),
            pl.BlockSpec((ds_, d), const2),
            pl.BlockSpec((1, d), const2),
            pl.BlockSpec((1, d), const2),
            pl.BlockSpec(memory_space=pl.ANY),
        ],
        out_specs=pl.BlockSpec((tm, d), lambda i: (i, 0)),
        out_shape=jax.ShapeDtypeStruct((t, d), F32),
        scratch_shapes=[
            pltpu.VMEM((2, TOP_K, tm * SUBLANES, LANES), U32),
            pltpu.VMEM((TOP_K, tm, LANES), F32),
            pltpu.SemaphoreType.DMA((2, TOP_K)),
        ],
        compiler_params=pltpu.CompilerParams(
            dimension_semantics=("arbitrary",), vmem_limit_bytes=VMEM_LIMIT_BYTES),
        name="final",
    )(dest, dest, gates_t, x1, h2p, ada3, ws1_b, ws3_b, ws2_b, ln2_g.reshape(1, d), ln2_b.reshape(1, d), ys)


def kernel(x, c, w_ada, b_ada, w_in, b_in, ln_v_g, ln_v_b, w_spatial, b_spatial, w_pool, b_pool, pool_scale, w_out, ln1_g, ln1_b, w_router, router_bias, w1, w3, w2, ws1, ws3, ws2, ln2_g, ln2_b):
    bsz, seq, d = x.shape
    t = bsz * seq
    e = w_router.shape[-1]
    assert w_ada.shape[0] == DEPTH
    l = 0

    ada3 = _ada_call(c, w_ada[l], b_ada[l]).reshape(bsz, 6, d)

    cat = _mix_call(x, ada3, w_in[l].astype(BF16), b_in[l], ln_v_g[l], ln_v_b[l], w_spatial[l],
                    b_spatial[l].T, w_pool[l].astype(BF16), b_pool[l], pool_scale[l])

    wr_t = w_router[l].T
    wr_hi = wr_t.astype(BF16)
    wr_lo = (wr_t - wr_hi.astype(F32)).astype(BF16)
    x1, h2p, logits_t = _out_call(cat.reshape(t, d), x.reshape(t, d), ada3, w_out[l].astype(BF16),
                                  ln1_g[l], ln1_b[l], wr_hi, wr_lo, seq)

    idx, gates, rank, cnt = _route_call(logits_t, router_bias[l])

    tb = GMM_TB
    nb = (t * TOP_K) // tb + e
    dest, last, be, nxt, par, nblk = _plan_call(cnt, idx, rank, nb, tb)

    xs = _dispatch_call(dest, h2p, _padzero_call(last[:, 0], nb, tb))

    ys = _gmm_call(be[0], nxt[:, 0, :], par[0], nblk[0, :1], xs, w1.reshape(w1.shape[1:]),
                   w3.reshape(w3.shape[1:]), w2.reshape(w2.shape[1:]))

    out = _final_call(dest, gates.T, x1, h2p, ada3, ws1[l].astype(BF16), ws3[l].astype(BF16),
                      ws2[l].astype(BF16), ln2_g[l], ln2_b[l], ys, seq)
    return out.reshape(bsz, seq, d)
```
